```python
import math
import jax, jax.numpy as jnp
from jax import lax
import numpy as np

D_MODEL = 1024
BATCH = 2
SEQ = 8192
DEPTH = 2
DEC_BATCH = 32
DEC_SEQ = 2048
PAST_LEN = 128

N_EVEN = (DEPTH + 1) // 2
N_ODD = DEPTH // 2
EPS = 1e-6
BLOCK = 128

A_HEADS = 8
A_KV_HEADS = 2
A_HEAD_DIM = 64
A_WINDOW = 128
A_Q_DIM = A_HEADS * A_HEAD_DIM
A_KV_DIM = A_KV_HEADS * A_HEAD_DIM

B_GROUPS = 4
B_WIDTH = D_MODEL // 2
B_GROUP_DIM = B_WIDTH // B_GROUPS
B_POOL_SIZES = (2, 4, 8, 16)

IN_AB_DIM = A_Q_DIM + 2 * A_KV_DIM + B_WIDTH
OUT_AB_DIM = A_Q_DIM + B_WIDTH

C_HEADS = 16
C_NOPE = 64
C_ROPE = 32
C_QK = C_NOPE + C_ROPE
C_V = 64
C_Q_LORA = 256
C_KV_LORA = 128
C_IN_DIM = C_Q_LORA + C_KV_LORA + C_ROPE
ROPE_THETA = 10000.0

D_FF = 2816
MOE_EXPERTS = 8
MOE_TOP_K = 2
E_FF = 3584

PLE_DIM = 256

kernel_name = "hybrid_bidir_encoder_swa_pool_mla_moe"


def _rms_norm(x, g):
    xf = x.astype(jnp.float32)
    y = xf * lax.rsqrt(jnp.mean(xf * xf, axis=-1, keepdims=True) + EPS)
    return (y * g.astype(jnp.float32)).astype(x.dtype)


def _rope(x, pos):
    half = x.shape[-1] // 2
    inv = ROPE_THETA ** (-jnp.arange(half, dtype=jnp.float32) / half)
    ang = pos.astype(jnp.float32)[:, None] * inv[None, :]
    cos = jnp.cos(ang)[None, :, None, :]
    sin = jnp.sin(ang)[None, :, None, :]
    xf = x.astype(jnp.float32)
    x1, x2 = xf[..., :half], xf[..., half:]
    return jnp.concatenate([x1 * cos - x2 * sin, x1 * sin + x2 * cos], axis=-1).astype(x.dtype)


def _window_attention(q, k, v, sink):
    B, S, H, D = q.shape
    KVH = k.shape[2]
    G = H // KVH
    nb = S // BLOCK
    scale = D ** -0.5
    qb = q.reshape(B, nb, BLOCK, KVH, G, D).transpose(1, 0, 2, 3, 4, 5)

    def neighbours(t):
        tp = jnp.pad(t, ((0, 0), (BLOCK, BLOCK), (0, 0), (0, 0)))
        tb = tp.reshape(B, nb + 2, BLOCK, KVH, D)
        tw = jnp.concatenate([tb[:, :-2], tb[:, 1:-1], tb[:, 2:]], axis=2)
        return tw.transpose(1, 0, 2, 3, 4)

    kw = neighbours(k)
    vw = neighbours(v)
    slopes = 2.0 ** (-8.0 * jnp.arange(1, H + 1, dtype=jnp.float32) / H)
    qi = jnp.arange(BLOCK)[:, None]
    kj = jnp.arange(3 * BLOCK)[None, :]
    dist = jnp.abs(qi - kj + BLOCK)
    in_band = dist <= A_WINDOW
    bias = -slopes.reshape(KVH, G, 1, 1) * dist.astype(jnp.float32)
    sink_l = sink.astype(jnp.float32).reshape(1, KVH, G, 1, 1)

    def block(args):
        n, qn, kn, vn = args
        key_pos = n * BLOCK - BLOCK + jnp.arange(3 * BLOCK)
        valid = in_band & ((key_pos >= 0) & (key_pos < S))[None, :]
        s = jnp.einsum('bqkgd,bskd->bkgqs', qn, kn).astype(jnp.float32) * scale + bias
        s = jnp.where(valid, s, -jnp.inf)
        m = jnp.maximum(jnp.max(s, axis=-1, keepdims=True), sink_l)
        e = jnp.exp(s - m)
        pr = e / (jnp.sum(e, axis=-1, keepdims=True) + jnp.exp(sink_l - m))
        return jnp.einsum('bkgqs,bskd->bqkgd', pr.astype(vn.dtype), vn)

    o = lax.map(block, (jnp.arange(nb), qb, kw, vw))
    return o.transpose(1, 0, 2, 3, 4, 5).reshape(B, S, H * D)


def _multiscale_pool(u, w_pool, pool_scale):
    B, S, C = u.shape
    uf = u.astype(jnp.float32)
    csum = jnp.concatenate([jnp.zeros((B, 1, C), jnp.float32), jnp.cumsum(uf, axis=1)], axis=1)
    t = jnp.arange(S)
    means = []
    for g, w in enumerate(B_POOL_SIZES):
        half = w // 2
        lo = jnp.maximum(t - half, 0)
        hi = jnp.minimum(t + half, S)
        cg = csum[..., g * B_GROUP_DIM:(g + 1) * B_GROUP_DIM]
        total = jnp.take(cg, hi, axis=1) - jnp.take(cg, lo, axis=1)
        means.append(total / (hi - lo).astype(jnp.float32)[None, :, None])
    pooled = jnp.concatenate(means, axis=-1)
    d = (pooled - uf).astype(u.dtype).reshape(B, S, B_GROUPS, B_GROUP_DIM)
    y = jnp.einsum('bsgc,gcd->bsgd', d, w_pool).reshape(B, S, C)
    return y * pool_scale


def _mixer_ab(xn, w_in, q_norm, k_norm, sink, pool_w, pool_scale, w_out):
    B, S, _ = xn.shape
    a = xn @ w_in
    q = a[..., :A_Q_DIM].reshape(B, S, A_HEADS, A_HEAD_DIM)
    k = a[..., A_Q_DIM:A_Q_DIM + A_KV_DIM].reshape(B, S, A_KV_HEADS, A_HEAD_DIM)
    v = a[..., A_Q_DIM + A_KV_DIM:A_Q_DIM + 2 * A_KV_DIM].reshape(B, S, A_KV_HEADS, A_HEAD_DIM)
    u = a[..., A_Q_DIM + 2 * A_KV_DIM:]
    q = _rms_norm(q, q_norm)
    k = _rms_norm(k, k_norm)
    o_a = _window_attention(q, k, v, sink)
    o_b = _multiscale_pool(u, pool_w, pool_scale)
    return jnp.concatenate([o_a, o_b], axis=-1) @ w_out


def _dense_attention(q, k, v):
    B, S, H, Dq = q.shape
    Dv = v.shape[-1]
    nb = S // BLOCK
    scale = Dq ** -0.5
    qb = q.reshape(B, nb, BLOCK, H, Dq).transpose(1, 0, 2, 3, 4)

    def block(qn):
        s = jnp.einsum('bqhd,bshd->bhqs', qn, k).astype(jnp.float32) * scale
        pr = jax.nn.softmax(s, axis=-1)
        return jnp.einsum('bhqs,bshd->bqhd', pr.astype(v.dtype), v)

    o = lax.map(block, qb)
    return o.transpose(1, 0, 2, 3, 4).reshape(B, S, H * Dv)


def _mla(xn, w_in, q_lat_norm, kv_lat_norm, w_uq, w_ukv, q_norm, k_norm, w_out):
    B, S, _ = xn.shape
    a = xn @ w_in
    cq = _rms_norm(a[..., :C_Q_LORA], q_lat_norm)
    ckv = _rms_norm(a[..., C_Q_LORA:C_Q_LORA + C_KV_LORA], kv_lat_norm)
    k_pe = a[..., C_Q_LORA + C_KV_LORA:]
    q = (cq @ w_uq).reshape(B, S, C_HEADS, C_QK)
    kv = (ckv @ w_ukv).reshape(B, S, C_HEADS, C_NOPE + C_V)
    k_nope, v = kv[..., :C_NOPE], kv[..., C_NOPE:]
    k = jnp.concatenate([k_nope, jnp.broadcast_to(k_pe[:, :, None, :], (B, S, C_HEADS, C_ROPE))], axis=-1)
    q = _rms_norm(q, q_norm)
    k = _rms_norm(k, k_norm)
    pos = jnp.arange(S)
    q = jnp.concatenate([q[..., :C_NOPE], _rope(q[..., C_NOPE:], pos)], axis=-1)
    k = jnp.concatenate([k[..., :C_NOPE], _rope(k[..., C_NOPE:], pos)], axis=-1)
    o = _dense_attention(q, k, v)
    return o @ w_out


def _swiglu(xn, w_gate, w_up, w_down):
    return (jax.nn.silu(xn @ w_gate) * (xn @ w_up)) @ w_down


def _moe_swiglu(xn, router, w_gate, w_up, w_down):
    B, S, D = xn.shape
    xf = xn.reshape(B * S, D)
    logits = (xf @ router).astype(jnp.float32)
    top_v, top_i = lax.top_k(logits, MOE_TOP_K)
    top_w = jax.nn.softmax(top_v, axis=-1)
    gates = jnp.einsum('nk,nke->ne', top_w,
                       jax.nn.one_hot(top_i, MOE_EXPERTS, dtype=jnp.float32)).astype(xn.dtype)
    out = jnp.zeros_like(xf)
    for e in range(MOE_EXPERTS):
        hdn = jax.nn.silu(xf @ w_gate[e]) * (xf @ w_up[e])
        out = out + gates[:, e:e + 1] * (hdn @ w_down[e])
    return out.reshape(B, S, D)


def _trunk(x, p, W):
    h = x
    for i in range(DEPTH):
        j = i // 2
        xn = _rms_norm(h, W['norm_mix'][i])
        if i % 2 == 0:
            h = h + _mixer_ab(xn, W['ab_w_in'][j], W['ab_q_norm'][j], W['ab_k_norm'][j], W['ab_sink'][j],
                              W['ab_pool_w'][j], W['ab_pool_scale'][j], W['ab_w_out'][j])
            xn = _rms_norm(h, W['norm_ffn'][i])
            h = h + _swiglu(xn, W['ffn_w_gate'][j], W['ffn_w_up'][j], W['ffn_w_down'][j])
        else:
            h = h + _mla(xn, W['mla_w_in'][j], W['mla_q_lat_norm'][j], W['mla_kv_lat_norm'][j],
                         W['mla_w_uq'][j], W['mla_w_ukv'][j], W['mla_q_norm'][j], W['mla_k_norm'][j],
                         W['mla_w_out'][j])
            xn = _rms_norm(h, W['norm_ffn'][i])
            h = h + _moe_swiglu(xn, W['moe_router'][j], W['moe_w_gate'][j], W['moe_w_up'][j],
                                W['moe_w_down'][j])
        gate = jax.nn.sigmoid(_rms_norm(h, W['ple_gate_norm'][i]) @ W['ple_w_gate'][i])
        h = h + gate * (p[i] @ W['ple_w_proj'][i])
    return h


def setup_inputs(seed: int = 0) -> dict:
    key = jax.random.key(seed)
    ks = jax.random.split(key, 32)
    f32 = jnp.float32

    def nrm(k, shape, scale):
        return jax.random.normal(k, shape, f32) * scale

    def gain(k, shape):
        return 1.0 + 0.05 * jax.random.normal(k, shape, f32)

    D = D_MODEL
    return {
        'x_prompt': nrm(ks[0], (BATCH, SEQ, D), 1.0),
        'x_sample': nrm(ks[1], (DEC_BATCH, DEC_SEQ, D), 1.0),
        'p_prompt': nrm(ks[2], (DEPTH, BATCH, SEQ, PLE_DIM), 1.0),
        'p_sample': nrm(ks[3], (DEPTH, DEC_BATCH, DEC_SEQ, PLE_DIM), 1.0),
        'norm_mix': gain(ks[4], (DEPTH, D)),
        'norm_ffn': gain(ks[5], (DEPTH, D)),
        'ab_w_in': nrm(ks[6], (N_EVEN, D, IN_AB_DIM), D ** -0.5),
        'ab_q_norm': gain(ks[7], (N_EVEN, A_HEAD_DIM)),
        'ab_k_norm': gain(ks[8], (N_EVEN, A_HEAD_DIM)),
        'ab_sink': nrm(ks[9], (N_EVEN, A_HEADS), 1.0),
        'ab_pool_w': nrm(ks[10], (N_EVEN, B_GROUPS, B_GROUP_DIM, B_GROUP_DIM), B_GROUP_DIM ** -0.5),
        'ab_pool_scale': gain(ks[11], (N_EVEN, B_WIDTH)),
        'ab_w_out': nrm(ks[12], (N_EVEN, OUT_AB_DIM, D), OUT_AB_DIM ** -0.5),
        'ffn_w_gate': nrm(ks[13], (N_EVEN, D, D_FF), D ** -0.5),
        'ffn_w_up': nrm(ks[14], (N_EVEN, D, D_FF), D ** -0.5),
        'ffn_w_down': nrm(ks[15], (N_EVEN, D_FF, D), D_FF ** -0.5),
        'mla_w_in': nrm(ks[16], (N_ODD, D, C_IN_DIM), D ** -0.5),
        'mla_q_lat_norm': gain(ks[17], (N_ODD, C_Q_LORA)),
        'mla_kv_lat_norm': gain(ks[18], (N_ODD, C_KV_LORA)),
        'mla_w_uq': nrm(ks[19], (N_ODD, C_Q_LORA, C_HEADS * C_QK), C_Q_LORA ** -0.5),
        'mla_w_ukv': nrm(ks[20], (N_ODD, C_KV_LORA, C_HEADS * (C_NOPE + C_V)), C_KV_LORA ** -0.5),
        'mla_q_norm': gain(ks[21], (N_ODD, C_QK)),
        'mla_k_norm': gain(ks[22], (N_ODD, C_QK)),
        'mla_w_out': nrm(ks[23], (N_ODD, C_HEADS * C_V, D), (C_HEADS * C_V) ** -0.5),
        'moe_router': nrm(ks[24], (N_ODD, D, MOE_EXPERTS), D ** -0.5),
        'moe_w_gate': nrm(ks[25], (N_ODD, MOE_EXPERTS, D, E_FF), D ** -0.5),
        'moe_w_up': nrm(ks[26], (N_ODD, MOE_EXPERTS, D, E_FF), D ** -0.5),
        'moe_w_down': nrm(ks[27], (N_ODD, MOE_EXPERTS, E_FF, D), E_FF ** -0.5),
        'ple_w_proj': nrm(ks[28], (DEPTH, PLE_DIM, D), PLE_DIM ** -0.5),
        'ple_gate_norm': gain(ks[29], (DEPTH, D)),
        'ple_w_gate': nrm(ks[30], (DEPTH, D, D), D ** -0.5),
    }


def reference(x_prompt, x_sample, p_prompt, p_sample, norm_mix, norm_ffn,
              ab_w_in, ab_q_norm, ab_k_norm, ab_sink, ab_pool_w, ab_pool_scale, ab_w_out,
              ffn_w_gate, ffn_w_up, ffn_w_down,
              mla_w_in, mla_q_lat_norm, mla_kv_lat_norm, mla_w_uq, mla_w_ukv, mla_q_norm, mla_k_norm,
              mla_w_out, moe_router, moe_w_gate, moe_w_up, moe_w_down,
              ple_w_proj, ple_gate_norm, ple_w_gate):
    W = dict(norm_mix=norm_mix, norm_ffn=norm_ffn,
             ab_w_in=ab_w_in, ab_q_norm=ab_q_norm, ab_k_norm=ab_k_norm, ab_sink=ab_sink,
             ab_pool_w=ab_pool_w, ab_pool_scale=ab_pool_scale, ab_w_out=ab_w_out,
             ffn_w_gate=ffn_w_gate, ffn_w_up=ffn_w_up, ffn_w_down=ffn_w_down,
             mla_w_in=mla_w_in, mla_q_lat_norm=mla_q_lat_norm, mla_kv_lat_norm=mla_kv_lat_norm,
             mla_w_uq=mla_w_uq, mla_w_ukv=mla_w_ukv, mla_q_norm=mla_q_norm, mla_k_norm=mla_k_norm,
             mla_w_out=mla_w_out, moe_router=moe_router, moe_w_gate=moe_w_gate, moe_w_up=moe_w_up,
             moe_w_down=moe_w_down, ple_w_proj=ple_w_proj, ple_gate_norm=ple_gate_norm,
             ple_w_gate=ple_w_gate)
    y_prompt = _trunk(x_prompt, p_prompt, W)
    y_sample = _trunk(x_sample, p_sample, W)
    return (y_prompt, y_sample)
```

```python
import functools
import math

import numpy as np
import jax
import jax.numpy as jnp
from jax import lax
from jax.experimental import pallas as pl
from jax.experimental.pallas import tpu as pltpu

F32 = jnp.float32
BF16 = jnp.bfloat16

D_MODEL = 1024
EPS = 1e-6
LANES = 128
VMEM_LIMIT = 56 * 1024 * 1024

A_HEADS, A_KV_HEADS, A_HEAD_DIM, A_WINDOW = 8, 2, 64, 128
A_Q_DIM = A_HEADS * A_HEAD_DIM
B_WIDTH, B_GROUPS, B_GROUP_DIM = 512, 4, 128
B_POOL_SIZES = (2, 4, 8, 16)
POOL_HALO = 16
WIN_BLOCK = 128
C_HEADS, C_NOPE, C_ROPE, C_V = 16, 64, 32, 64
C_QK = C_NOPE + C_ROPE
C_Q_LORA, C_KV_LORA = 256, 128
ROPE_THETA = 10000.0
D_FF, MOE_EXPERTS, E_FF = 2816, 8, 3584
PLE_DIM = 256


def _const_spec(shape):
    nd = len(shape)
    return pl.BlockSpec(shape, lambda *_: (0,) * nd, pipeline_mode=pl.Buffered(1))


def _params(sem):
    return pltpu.CompilerParams(dimension_semantics=sem, vmem_limit_bytes=VMEM_LIMIT)


def _rms(x, g):
    ms = jnp.mean(x * x, axis=-1, keepdims=True)
    return x * lax.rsqrt(ms + EPS) * g


def _dot(a, b):
    return jnp.dot(a, b, preferred_element_type=F32)


def _dot_nt(a, b):
    return lax.dot_general(a, b, (((1,), (1,)), ((), ())), preferred_element_type=F32)


def _dot_tn(a, b):
    return lax.dot_general(a, b, (((0,), (0,)), ((), ())), preferred_element_type=F32)


def _ple(h, p, gnorm, w_gate, w_proj):
    gate = jax.nn.sigmoid(_dot(_rms(h, gnorm).astype(BF16), w_gate))
    return h + gate * _dot(p.astype(BF16), w_proj)


def _ab_in_kernel(h_ref, g_ref, w_ref, e_ref, qg_ref, kg_ref, q_ref, k_ref, v_ref, u_ref):
    xn = _rms(h_ref[0], g_ref[...]).astype(BF16)
    a = _dot(xn, w_ref[...])
    e = e_ref[...]

    def head_norm(t, g):
        ss = _dot((t * t).astype(BF16), e)
        return t * lax.rsqrt(ss * (1.0 / A_HEAD_DIM) + EPS) * g

    q_ref[0, :, 0:256] = head_norm(a[:, 0:256], qg_ref[:, 0:256]).astype(BF16)
    q_ref[0, :, 256:512] = head_norm(a[:, 256:512], qg_ref[:, 256:512]).astype(BF16)
    k_ref[0] = head_norm(a[:, 512:768], kg_ref[...]).astype(BF16)
    v_ref[0] = a[:, 768:1024].astype(BF16)
    u_ref[0] = a[:, 1024:1536].astype(BF16)


def _ab_in(h, g, w, e, qg, kg, tm):
    B, S, D = h.shape
    tok = lambda w_: pl.BlockSpec((1, tm, w_), lambda b, i: (b, i, 0))
    return pl.pallas_call(
        _ab_in_kernel,
        grid=(B, S // tm),
        in_specs=[tok(D), _const_spec(g.shape), _const_spec(w.shape), _const_spec(e.shape),
                  _const_spec(qg.shape), _const_spec(kg.shape)],
        out_specs=[tok(512), tok(256), tok(256), tok(512)],
        out_shape=[jax.ShapeDtypeStruct((B, S, 512), BF16), jax.ShapeDtypeStruct((B, S, 256), BF16),
                   jax.ShapeDtypeStruct((B, S, 256), BF16), jax.ShapeDtypeStruct((B, S, 512), BF16)],
        compiler_params=_params(("parallel", "parallel")),
        name="ab_in",
    )(h, g, w, e, qg, kg)


def _win_pool_kernel(sink_ref, q_ref, kp_ref, km_ref, kn_ref, vp_ref, vm_ref, vn_ref,
                     up_ref, um_ref, un_ref, pw_ref, ps_ref, o_ref, kbuf, vbuf, ubuf, *, seq, tq):
    j = pl.program_id(1)
    nj = pl.num_programs(1)
    W = WIN_BLOCK
    kbuf[0:W] = kp_ref[0]
    kbuf[W:W + tq] = km_ref[0]
    kbuf[W + tq:] = kn_ref[0]
    vbuf[0:W] = vp_ref[0]
    vbuf[W:W + tq] = vm_ref[0]
    vbuf[W + tq:] = vn_ref[0]

    lane = lax.broadcasted_iota(jnp.int32, (1, 2 * LANES), 1) % LANES
    lo = lane < A_HEAD_DIM
    kb = kbuf[...]
    vb = vbuf[...]
    zero = jnp.zeros_like(kb)
    k_par = (jnp.where(lo, kb, zero), jnp.where(lo, zero, kb))
    v_par = (jnp.where(lo, vb, zero), jnp.where(lo, zero, vb))

    qi = lax.broadcasted_iota(jnp.int32, (W, 3 * W), 0)
    kj = lax.broadcasted_iota(jnp.int32, (W, 3 * W), 1)
    dist = jnp.abs(qi - kj + W)
    distf = dist.astype(F32)
    in_band = dist <= A_WINDOW
    G = A_HEADS // A_KV_HEADS
    for sb in range(tq // W):
        key_pos = j * tq + (sb - 1) * W + kj
        valid = in_band & (key_pos >= 0) & (key_pos < seq)
        neg = jnp.where(valid, 0.0, -jnp.inf)
        rows = slice(sb * W, (sb + 3) * W)
        for pair in range(A_HEADS // 2):
            g = (2 * pair) // G
            cols = slice(g * LANES, (g + 1) * LANES)
            qp = q_ref[0, sb * W:(sb + 1) * W, pair * LANES:(pair + 1) * LANES]
            o = jnp.zeros((W, LANES), F32)
            for par in range(2):
                h = 2 * pair + par
                slope = 2.0 ** (-8.0 * (h + 1) / A_HEADS)
                sink = sink_ref[h]
                s = _dot_nt(qp, k_par[par][rows, cols]) - slope * distf + neg
                m = jnp.maximum(jnp.max(s, axis=-1, keepdims=True), sink)
                ex = jnp.exp(s - m)
                den = jnp.sum(ex, axis=-1, keepdims=True) + jnp.exp(sink - m)
                pr = (ex / den).astype(BF16)
                o = o + _dot(pr, v_par[par][rows, cols])
            o_ref[0, sb * W:(sb + 1) * W, pair * LANES:(pair + 1) * LANES] = o.astype(BF16)

    H = POOL_HALO
    ubuf[0:H] = jnp.where(j > 0, up_ref[0].astype(F32), 0.0)
    ubuf[H:H + tq] = um_ref[0].astype(F32)
    ubuf[H + tq:] = jnp.where(j < nj - 1, un_ref[0].astype(F32), 0.0)
    t = j * tq + lax.broadcasted_iota(jnp.int32, (tq, 1), 0)
    for g, w in enumerate(B_POOL_SIZES):
        half = w // 2
        cols = slice(g * B_GROUP_DIM, (g + 1) * B_GROUP_DIM)
        tot = jnp.zeros((tq, B_GROUP_DIM), F32)
        for d in range(-half, half):
            tot = tot + ubuf[H + d:H + d + tq, cols]
        cnt = (jnp.minimum(t + half, seq) - jnp.maximum(t - half, 0)).astype(F32)
        dlt = (tot / cnt - ubuf[H:H + tq, cols]).astype(BF16)
        y = _dot(dlt, pw_ref[g]) * ps_ref[:, cols]
        o_ref[0, :, A_Q_DIM + g * B_GROUP_DIM:A_Q_DIM + (g + 1) * B_GROUP_DIM] = y.astype(BF16)


def _win_pool(q, k, v, u, sink, pool_w, pool_scale, tq):
    B, S, _ = q.shape
    W, H = WIN_BLOCK, POOL_HALO
    nw, nh = S // W, S // H
    rw, rh = tq // W, tq // H
    main = lambda w_: pl.BlockSpec((1, tq, w_), lambda b, j: (b, j, 0))
    prev = lambda rows, r, w_: pl.BlockSpec((1, rows, w_), lambda b, j: (b, jnp.maximum(j * r - 1, 0), 0))
    nxt = lambda rows, r, n, w_: pl.BlockSpec((1, rows, w_), lambda b, j: (b, jnp.minimum((j + 1) * r, n - 1), 0))
    kernel = functools.partial(_win_pool_kernel, seq=S, tq=tq)
    return pl.pallas_call(
        kernel,
        grid=(B, S // tq),
        in_specs=[pl.BlockSpec(memory_space=pltpu.SMEM), main(512),
                  prev(W, rw, 256), main(256), nxt(W, rw, nw, 256),
                  prev(W, rw, 256), main(256), nxt(W, rw, nw, 256),
                  prev(H, rh, 512), main(512), nxt(H, rh, nh, 512),
                  _const_spec(pool_w.shape), _const_spec(pool_scale.shape)],
        out_specs=main(1024),
        out_shape=jax.ShapeDtypeStruct((B, S, 1024), BF16),
        scratch_shapes=[pltpu.VMEM((tq + 2 * W, 256), BF16), pltpu.VMEM((tq + 2 * W, 256), BF16),
                        pltpu.VMEM((tq + 2 * H, 512), F32)],
        compiler_params=_params(("parallel", "parallel")),
        name="win_pool",
    )(sink, q, k, k, k, v, v, v, u, u, u, pool_w, pool_scale)


def _out_ffn_ple_kernel(h_ref, o_ref, p_ref, wo_ref, nf_ref, wg_ref, wu_ref, wd_ref,
                        gn_ref, pg_ref, pp_ref, out_ref, *, ff_chunk):
    h1 = h_ref[0] + _dot(o_ref[0], wo_ref[...])
    xn = _rms(h1, nf_ref[...]).astype(BF16)
    acc = jnp.zeros_like(h1)
    for c0 in range(0, D_FF, ff_chunk):
        g = _dot(xn, wg_ref[:, c0:c0 + ff_chunk])
        u = _dot(xn, wu_ref[:, c0:c0 + ff_chunk])
        hdn = (g * jax.nn.sigmoid(g) * u).astype(BF16)
        acc = acc + _dot(hdn, wd_ref[c0:c0 + ff_chunk, :])
    out_ref[0] = _ple(h1 + acc, p_ref[0], gn_ref[...], pg_ref[...], pp_ref[...])


def _out_ffn_ple(h, o, p, wo, nf, wg, wu, wd, gn, pg, pp, tm):
    B, S, D = h.shape
    tok = lambda w_: pl.BlockSpec((1, tm, w_), lambda b, i: (b, i, 0))
    consts = [wo, nf, wg, wu, wd, gn, pg, pp]
    return pl.pallas_call(
        functools.partial(_out_ffn_ple_kernel, ff_chunk=D_FF // 2),
        grid=(B, S // tm),
        in_specs=[tok(D), tok(D), tok(PLE_DIM)] + [_const_spec(c.shape) for c in consts],
        out_specs=tok(D),
        out_shape=jax.ShapeDtypeStruct((B, S, D), F32),
        compiler_params=_params(("parallel", "parallel")),
        name="out_ffn_ple",
    )(h, o, p, *consts)


def _mla_in_kernel(h_ref, g_ref, w_ref, qln_ref, kln_ref, wuq_ref, gq_ref, cosT_ref, sinT_ref,
                   wuk_ref, wuv_ref, gkn_ref, gkr_ref, cos_ref, sin_ref, rot_ref, place_ref, e_ref,
                   qT_ref, k_ref, vT_ref):
    xn = _rms(h_ref[0], g_ref[...]).astype(BF16)
    a = _dot(xn, w_ref[...])
    cq = _rms(a[:, 0:C_Q_LORA], qln_ref[...]).astype(BF16)
    ckv = _rms(a[:, C_Q_LORA:C_Q_LORA + C_KV_LORA], kln_ref[...]).astype(BF16)
    kpe = a[:, C_Q_LORA + C_KV_LORA:]

    qT = _dot_nt(wuq_ref[...], cq)
    cosT = cosT_ref[...]
    sinT = sinT_ref[...]
    half = C_ROPE // 2
    for h in range(C_HEADS):
        blk = qT[h * LANES:(h + 1) * LANES]
        ss = jnp.sum(blk * blk, axis=0, keepdims=True)
        qn = blk * lax.rsqrt(ss * (1.0 / C_QK) + EPS) * gq_ref[h * LANES:(h + 1) * LANES]
        x1 = qn[C_NOPE:C_NOPE + half]
        x2 = qn[C_NOPE + half:C_QK]
        out = jnp.concatenate([qn[0:C_NOPE], x1 * cosT - x2 * sinT, x1 * sinT + x2 * cosT, qn[C_QK:]], axis=0)
        qT_ref[0, h] = out.astype(BF16)

    vT = _dot_nt(wuv_ref[...], ckv)
    for h in range(C_HEADS):
        vT_ref[0, h] = vT[h * C_V:(h + 1) * C_V].astype(BF16)

    kraw = _dot(ckv, wuk_ref[...])
    kg = kpe * gkr_ref[...]
    krope = kg * cos_ref[...] + _dot(kg.astype(BF16), rot_ref[...]) * sin_ref[...]
    placed = _dot(krope.astype(BF16), place_ref[...])
    ss_pe = jnp.sum(kpe * kpe, axis=-1, keepdims=True)
    e = e_ref[...]
    for h2 in range(C_HEADS // 2):
        cols = slice(h2 * 2 * LANES, (h2 + 1) * 2 * LANES)
        kr = kraw[:, cols]
        ss = _dot((kr * kr).astype(BF16), e) + ss_pe
        kn = (kr * gkn_ref[:, cols] + placed[:, cols]) * lax.rsqrt(ss * (1.0 / C_QK) + EPS)
        k_ref[0, 2 * h2] = kn[:, 0:LANES].astype(BF16)
        k_ref[0, 2 * h2 + 1] = kn[:, LANES:].astype(BF16)


def _mla_in(h, g, w, qln, kln, wuq, gq, cosT, sinT, wuk, wuv, gkn, gkr, cos, sin, rot, place, e, tm):
    B, S, D = h.shape
    consts_a = [g, w, qln, kln, wuq, gq]
    consts_b = [wuk, wuv, gkn, gkr]
    consts_c = [rot, place, e]
    return pl.pallas_call(
        _mla_in_kernel,
        grid=(B, S // tm),
        in_specs=[pl.BlockSpec((1, tm, D), lambda b, i: (b, i, 0))]
        + [_const_spec(c.shape) for c in consts_a]
        + [pl.BlockSpec((C_ROPE // 2, tm), lambda b, i: (0, i))] * 2
        + [_const_spec(c.shape) for c in consts_b]
        + [pl.BlockSpec((tm, LANES), lambda b, i: (i, 0))] * 2
        + [_const_spec(c.shape) for c in consts_c],
        out_specs=[pl.BlockSpec((1, C_HEADS, LANES, tm), lambda b, i: (b, 0, 0, i)),
                   pl.BlockSpec((1, C_HEADS, tm, LANES), lambda b, i: (b, 0, i, 0)),
                   pl.BlockSpec((1, C_HEADS, C_V, tm), lambda b, i: (b, 0, 0, i))],
        out_shape=[jax.ShapeDtypeStruct((B, C_HEADS, LANES, S), BF16),
                   jax.ShapeDtypeStruct((B, C_HEADS, S, LANES), BF16),
                   jax.ShapeDtypeStruct((B, C_HEADS, C_V, S), BF16)],
        compiler_params=_params(("parallel", "parallel")),
        name="mla_in",
    )(h, *consts_a, cosT, sinT, *consts_b, cos, sin, *consts_c)


def _mla_attn_kernel(qT_ref, k_ref, vT_ref, oT_ref, m_sc, l_sc, acc_sc, *, tk):
    seq = k_ref.shape[2]
    qT = qT_ref[0, 0]
    m_sc[...] = jnp.full_like(m_sc, -jnp.inf)
    l_sc[...] = jnp.zeros_like(l_sc)
    acc_sc[...] = jnp.zeros_like(acc_sc)

    def body(c, carry):
        off = pl.multiple_of(c * tk, tk)
        s = _dot(k_ref[0, 0, pl.ds(off, tk), :], qT)
        m_old = m_sc[...]
        m_new = jnp.maximum(m_old, jnp.max(s, axis=0, keepdims=True))
        p = jnp.exp(s - m_new)
        alpha = jnp.exp(m_old - m_new)
        l_sc[...] = alpha * l_sc[...] + jnp.sum(p, axis=0, keepdims=True)
        acc_sc[...] = alpha * acc_sc[...] + _dot(vT_ref[0, 0, :, pl.ds(off, tk)], p.astype(BF16))
        m_sc[...] = m_new
        return carry

    lax.fori_loop(0, seq // tk, body, 0)
    oT_ref[0, 0] = (acc_sc[...] / l_sc[...]).astype(BF16)


def _mla_attn(qT, k, vT, tq, tk):
    B, H, _, S = qT.shape
    return pl.pallas_call(
        functools.partial(_mla_attn_kernel, tk=tk),
        grid=(B, H, S // tq),
        in_specs=[pl.BlockSpec((1, 1, LANES, tq), lambda b, h, i: (b, h, 0, i)),
                  pl.BlockSpec((1, 1, S, LANES), lambda b, h, i: (b, h, 0, 0)),
                  pl.BlockSpec((1, 1, C_V, S), lambda b, h, i: (b, h, 0, 0))],
        out_specs=pl.BlockSpec((1, 1, C_V, tq), lambda b, h, i: (b, h, 0, i)),
        out_shape=jax.ShapeDtypeStruct((B, H, C_V, S), BF16),
        scratch_shapes=[pltpu.VMEM((1, tq), F32), pltpu.VMEM((1, tq), F32), pltpu.VMEM((C_V, tq), F32)],
        compiler_params=_params(("parallel", "parallel", "arbitrary")),
        name="mla_attn",
    )(qT, k, vT)


def _mla_out_router_kernel(h_ref, oT_ref, wo_ref, nf_ref, r_ref, h1_ref, xn_ref, route_ref):
    h1 = h_ref[0] + _dot_tn(oT_ref[0], wo_ref[...])
    h1_ref[0] = h1
    xn = _rms(h1, nf_ref[...])
    hi = xn.astype(BF16)
    xn_ref[0] = hi
    lo = (xn - hi.astype(F32)).astype(BF16)
    p_hi = _dot(hi, r_ref[...])
    p_lo = _dot(lo, r_ref[...])
    logits = p_hi + p_lo + pltpu.roll(p_hi, LANES - MOE_EXPERTS, axis=1)
    lane = lax.broadcasted_iota(jnp.int32, logits.shape, 1)
    logits = jnp.where(lane < MOE_EXPERTS, logits, -jnp.inf)
    m1 = jnp.max(logits, axis=-1, keepdims=True)
    i1 = jnp.min(jnp.where(logits == m1, lane, LANES), axis=-1, keepdims=True)
    rest = jnp.where(lane == i1, -jnp.inf, logits)
    m2 = jnp.max(rest, axis=-1, keepdims=True)
    i2 = jnp.min(jnp.where(rest == m2, lane, LANES), axis=-1, keepdims=True)
    e2 = jnp.exp(m2 - m1)
    w1 = 1.0 / (1.0 + e2)
    w2 = e2 * w1
    route = jnp.where(lane == 0, i1.astype(F32),
                      jnp.where(lane == 1, i2.astype(F32),
                                jnp.where(lane == 2, w1, jnp.where(lane == 3, w2, 0.0))))
    route_ref[0] = route


def _mla_out_router(h, oT, wo, nf, r, tm):
    B, S, D = h.shape
    tok = lambda w_: pl.BlockSpec((1, tm, w_), lambda b, i: (b, i, 0))
    return pl.pallas_call(
        _mla_out_router_kernel,
        grid=(B, S // tm),
        in_specs=[tok(D), pl.BlockSpec((1, D, tm), lambda b, i: (b, 0, i)),
                  _const_spec(wo.shape), _const_spec(nf.shape), _const_spec(r.shape)],
        out_specs=[tok(D), tok(D), tok(LANES)],
        out_shape=[jax.ShapeDtypeStruct((B, S, D), F32), jax.ShapeDtypeStruct((B, S, D), BF16),
                   jax.ShapeDtypeStruct((B, S, LANES), F32)],
        compiler_params=_params(("parallel", "parallel")),
        name="mla_out_router",
    )(h, oT, wo, nf, r)


def _experts_kernel(te_ref, tv_ref, x_ref, wg_ref, wu_ref, wd_ref, y_ref, acc_sc):
    i = pl.program_id(0)
    f = pl.program_id(1)

    @pl.when(tv_ref[i] > 0)
    def _():
        x = x_ref[...]
        g = _dot(x, wg_ref[0])
        u = _dot(x, wu_ref[0])
        hdn = (g * jax.nn.sigmoid(g) * u).astype(BF16)
        y = _dot(hdn, wd_ref[0])

        @pl.when(f == 0)
        def _():
            acc_sc[...] = y

        @pl.when(f == pl.num_programs(1) - 1)
        def _():
            y_ref[...] = (acc_sc[...] + y).astype(BF16)


def _experts(xs, te, tv, wg, wu, wd, tm, nf):
    P, D = xs.shape
    fc = E_FF // nf
    grid_spec = pltpu.PrefetchScalarGridSpec(
        num_scalar_prefetch=2,
        grid=(P // tm, nf),
        in_specs=[pl.BlockSpec((tm, D), lambda i, f, te, tv: (i, 0)),
                  pl.BlockSpec((1, D, fc), lambda i, f, te, tv: (te[i], 0, f)),
                  pl.BlockSpec((1, D, fc), lambda i, f, te, tv: (te[i], 0, f)),
                  pl.BlockSpec((1, fc, D), lambda i, f, te, tv: (te[i], f, 0))],
        out_specs=pl.BlockSpec((tm, D), lambda i, f, te, tv: (i, 0)),
        scratch_shapes=[pltpu.VMEM((tm, D), F32)],
    )
    return pl.pallas_call(
        _experts_kernel,
        grid_spec=grid_spec,
        out_shape=jax.ShapeDtypeStruct((P, D), BF16),
        compiler_params=_params(("parallel", "arbitrary")),
        name="experts",
    )(te, tv, xs, wg, wu, wd)


def _combine_ple_kernel(h_ref, y0_ref, y1_ref, r_ref, p_ref, gn_ref, pg_ref, pp_ref, out_ref):
    r = r_ref[0]
    h2 = h_ref[0] + r[:, 2:3] * y0_ref[0].astype(F32) + r[:, 3:4] * y1_ref[0].astype(F32)
    out_ref[0] = _ple(h2, p_ref[0], gn_ref[...], pg_ref[...], pp_ref[...])


def _combine_ple(h, y0, y1, route, p, gn, pg, pp, tm):
    B, S, D = h.shape
    tok = lambda w_: pl.BlockSpec((1, tm, w_), lambda b, i: (b, i, 0))
    consts = [gn, pg, pp]
    return pl.pallas_call(
        _combine_ple_kernel,
        grid=(B, S // tm),
        in_specs=[tok(D), tok(D), tok(D), tok(LANES), tok(PLE_DIM)] + [_const_spec(c.shape) for c in consts],
        out_specs=tok(D),
        out_shape=jax.ShapeDtypeStruct((B, S, D), F32),
        compiler_params=_params(("parallel", "parallel")),
        name="combine_ple",
    )(h, y0, y1, route, p, *consts)


def _block_diag_ones(n, blk):
    i = np.arange(n)
    return jnp.asarray((i[:, None] // blk == i[None, :] // blk).astype(np.float32), BF16)


def _prep_even(W, j, i):
    w = W['ab_w_in'][j]
    c = lambda a, b: w[:, a:b]
    k0, k1 = c(512, 576), c(576, 640)
    v0, v1 = c(640, 704), c(704, 768)
    w_in = jnp.concatenate([c(0, 512), k0, k0, k1, k1, v0, v0, v1, v1, c(768, 1280)], axis=1).astype(BF16)
    return dict(
        g=W['norm_mix'][i][None], w_in=w_in, e=_block_diag_ones(256, A_HEAD_DIM),
        qg=(jnp.tile(W['ab_q_norm'][j], A_HEADS) * A_HEAD_DIM ** -0.5)[None],
        kg=jnp.tile(W['ab_k_norm'][j], 2 * A_KV_HEADS)[None],
        sink=W['ab_sink'][j], pool_w=W['ab_pool_w'][j].astype(BF16), pool_scale=W['ab_pool_scale'][j][None],
        wo=W['ab_w_out'][j].astype(BF16), nf=W['norm_ffn'][i][None],
        wg=W['ffn_w_gate'][j].astype(BF16), wu=W['ffn_w_up'][j].astype(BF16), wd=W['ffn_w_down'][j].astype(BF16),
        gn=W['ple_gate_norm'][i][None], pg=W['ple_w_gate'][i].astype(BF16), pp=W['ple_w_proj'][i].astype(BF16))


def _prep_odd(W, j, i):
    pad_h = lambda a, n: jnp.pad(a, [(0, 0)] * (a.ndim - 1) + [(0, n - a.shape[-1])])
    w_in = pad_h(W['mla_w_in'][j], 512).astype(BF16)
    wuq = pad_h(W['mla_w_uq'][j].reshape(C_Q_LORA, C_HEADS, C_QK), LANES).reshape(C_Q_LORA, C_HEADS * LANES)
    gq = jnp.tile(pad_h(W['mla_q_norm'][j], LANES), C_HEADS) * C_QK ** -0.5
    ukv = W['mla_w_ukv'][j].reshape(C_KV_LORA, C_HEADS, C_NOPE + C_V)
    wuk = pad_h(ukv[..., :C_NOPE], LANES).reshape(C_KV_LORA, C_HEADS * LANES)
    wuv = ukv[..., C_NOPE:].reshape(C_KV_LORA, C_HEADS * C_V)
    kn = W['mla_k_norm'][j]
    gkn = jnp.tile(pad_h(kn[:C_NOPE], LANES), C_HEADS)[None]
    gkr = pad_h(kn[C_NOPE:], LANES)[None]
    half = C_ROPE // 2
    rot = np.zeros((LANES, LANES), np.float32)
    rot[np.arange(half) + half, np.arange(half)] = -1.0
    rot[np.arange(half), np.arange(half) + half] = 1.0
    place = np.zeros((LANES, C_HEADS * LANES), np.float32)
    for h in range(C_HEADS):
        place[np.arange(C_ROPE), h * LANES + C_NOPE + np.arange(C_ROPE)] = 1.0
    router = W['moe_router'][j]
    r_hi = router.astype(BF16)
    r_lo = (router - r_hi.astype(F32)).astype(BF16)
    r = pad_h(jnp.concatenate([r_hi, r_lo], axis=1), LANES)
    return dict(
        g=W['norm_mix'][i][None], w_in=w_in, qln=W['mla_q_lat_norm'][j][None], kln=W['mla_kv_lat_norm'][j][None],
        wuq=wuq.T.astype(BF16), gq=gq[:, None], wuk=wuk.astype(BF16), wuv=wuv.T.astype(BF16), gkn=gkn, gkr=gkr,
        rot=jnp.asarray(rot, BF16), place=jnp.asarray(place, BF16), e=_block_diag_ones(2 * LANES, LANES),
        wo=W['mla_w_out'][j].astype(BF16), nf=W['norm_ffn'][i][None], r=r,
        wg=W['moe_w_gate'][j].astype(BF16), wu=W['moe_w_up'][j].astype(BF16), wd=W['moe_w_down'][j].astype(BF16),
        gn=W['ple_gate_norm'][i][None], pg=W['ple_w_gate'][i].astype(BF16), pp=W['ple_w_proj'][i].astype(BF16))


def _rope_tables(S):
    half = C_ROPE // 2
    inv = ROPE_THETA ** (-jnp.arange(half, dtype=F32) / half)
    ang = jnp.arange(S).astype(F32)[:, None] * inv[None, :]
    cos, sin = jnp.cos(ang), jnp.sin(ang)
    wide = lambda t: jnp.pad(jnp.concatenate([t, t], axis=1), ((0, 0), (0, LANES - C_ROPE)))
    return cos.T, sin.T, wide(cos), wide(sin)


def _route_layout(ids, tm):
    n_slots = ids.size
    n_tiles = n_slots // tm + MOE_EXPERTS
    flat = ids.reshape(-1)
    order = jnp.argsort(flat, stable=True)
    counts = jnp.sum(flat[:, None] == jnp.arange(MOE_EXPERTS)[None, :], axis=0)
    tiles = (counts + tm - 1) // tm
    tile_start = jnp.cumsum(tiles) - tiles
    slot_start = jnp.cumsum(counts) - counts
    e_sorted = flat[order]
    row_sorted = tile_start[e_sorted] * tm + jnp.arange(n_slots) - slot_start[e_sorted]
    src = jnp.zeros((n_tiles * tm,), jnp.int32).at[row_sorted].set((order // 2).astype(jnp.int32))
    row_of_slot = jnp.zeros((n_slots,), jnp.int32).at[order].set(row_sorted.astype(jnp.int32))
    tile_idx = jnp.arange(n_tiles)
    te = jnp.sum(tile_idx[:, None] >= jnp.cumsum(tiles)[None, :], axis=1)
    tv = (te < MOE_EXPERTS).astype(jnp.int32)
    te = jnp.minimum(te, MOE_EXPERTS - 1).astype(jnp.int32)
    return src, te, tv, row_of_slot.reshape(ids.shape)


def _tiles(S):
    return dict(tm=min(512, S), tq_win=min(512, S), tq=min(1024, S), tk=min(512, S), te=512, nf=2)


def _trunk(x, p, even, odd):
    B, S, D = x.shape
    t = _tiles(S)
    q, k, v, u = _ab_in(x, even['g'], even['w_in'], even['e'], even['qg'], even['kg'], t['tm'])
    o = _win_pool(q, k, v, u, even['sink'], even['pool_w'], even['pool_scale'], t['tq_win'])
    h = _out_ffn_ple(x, o, p[0], even['wo'], even['nf'], even['wg'], even['wu'], even['wd'],
                     even['gn'], even['pg'], even['pp'], t['tm'])
    cosT, sinT, cos, sin = _rope_tables(S)
    qT, kk, vT = _mla_in(h, odd['g'], odd['w_in'], odd['qln'], odd['kln'], odd['wuq'], odd['gq'], cosT, sinT,
                         odd['wuk'], odd['wuv'], odd['gkn'], odd['gkr'], cos, sin, odd['rot'], odd['place'],
                         odd['e'], t['tm'])
    oT = _mla_attn(qT, kk, vT, t['tq'], t['tk'])
    h1, xn, route = _mla_out_router(h, oT.reshape(B, C_HEADS * C_V, S), odd['wo'], odd['nf'], odd['r'], t['tm'])
    ids = route[..., 0:2].astype(jnp.int32).reshape(B * S, 2)
    src, te, tv, rows = _route_layout(ids, t['te'])
    xs = jnp.take(xn.reshape(B * S, D), src, axis=0)
    ys = _experts(xs, te, tv, odd['wg'], odd['wu'], odd['wd'], t['te'], t['nf'])
    y0 = jnp.take(ys, rows[:, 0], axis=0).reshape(B, S, D)
    y1 = jnp.take(ys, rows[:, 1], axis=0).reshape(B, S, D)
    return _combine_ple(h1, y0, y1, route, p[1], odd['gn'], odd['pg'], odd['pp'], t['tm'])


def kernel(x_prompt, x_sample, p_prompt, p_sample, norm_mix, norm_ffn, ab_w_in, ab_q_norm, ab_k_norm, ab_sink, ab_pool_w, ab_pool_scale, ab_w_out, ffn_w_gate, ffn_w_up, ffn_w_down, mla_w_in, mla_q_lat_norm, mla_kv_lat_norm, mla_w_uq, mla_w_ukv, mla_q_norm, mla_k_norm, mla_w_out, moe_router, moe_w_gate, moe_w_up, moe_w_down, ple_w_proj, ple_gate_norm, ple_w_gate):
    W = dict(norm_mix=norm_mix, norm_ffn=norm_ffn,
             ab_w_in=ab_w_in, ab_q_norm=ab_q_norm, ab_k_norm=ab_k_norm, ab_sink=ab_sink,
             ab_pool_w=ab_pool_w, ab_pool_scale=ab_pool_scale, ab_w_out=ab_w_out,
             ffn_w_gate=ffn_w_gate, ffn_w_up=ffn_w_up, ffn_w_down=ffn_w_down,
             mla_w_in=mla_w_in, mla_q_lat_norm=mla_q_lat_norm, mla_kv_lat_norm=mla_kv_lat_norm,
             mla_w_uq=mla_w_uq, mla_w_ukv=mla_w_ukv, mla_q_norm=mla_q_norm, mla_k_norm=mla_k_norm,
             mla_w_out=mla_w_out, moe_router=moe_router, moe_w_gate=moe_w_gate, moe_w_up=moe_w_up,
             moe_w_down=moe_w_down, ple_w_proj=ple_w_proj, ple_gate_norm=ple_gate_norm,
             ple_w_gate=ple_w_gate)
    even = _prep_even(W, 0, 0)
    odd = _prep_odd(W, 0, 1)
    return (_trunk(x_prompt, p_prompt, even, odd), _trunk(x_sample, p_sample, even, odd))
```

```python
import functools
import math

import numpy as np
import jax
import jax.numpy as jnp
from jax import lax
from jax.experimental import pallas as pl
from jax.experimental.pallas import tpu as pltpu

F32 = jnp.float32
BF16 = jnp.bfloat16

D_MODEL = 1024
EPS = 1e-6
LANES = 128
VMEM_LIMIT = 56 * 1024 * 1024

A_HEADS, A_KV_HEADS, A_HEAD_DIM, A_WINDOW = 8, 2, 64, 128
A_Q_DIM = A_HEADS * A_HEAD_DIM
B_WIDTH, B_GROUPS, B_GROUP_DIM = 512, 4, 128
B_POOL_SIZES = (2, 4, 8, 16)
POOL_HALO = 16
WIN_BLOCK = 128
C_HEADS, C_NOPE, C_ROPE, C_V = 16, 64, 32, 64
C_QK = C_NOPE + C_ROPE
C_Q_LORA, C_KV_LORA = 256, 128
ROPE_THETA = 10000.0
D_FF, MOE_EXPERTS, E_FF = 2816, 8, 3584
PLE_DIM = 256


def _const_spec(shape):
    nd = len(shape)
    return pl.BlockSpec(shape, lambda *_: (0,) * nd, pipeline_mode=pl.Buffered(1))


def _params(sem):
    return pltpu.CompilerParams(dimension_semantics=sem, vmem_limit_bytes=VMEM_LIMIT)


def _rms(x, g):
    ms = jnp.mean(x * x, axis=-1, keepdims=True)
    return x * lax.rsqrt(ms + EPS) * g


def _dot(a, b):
    return jnp.dot(a, b, preferred_element_type=F32)


def _dot_nt(a, b):
    return lax.dot_general(a, b, (((1,), (1,)), ((), ())), preferred_element_type=F32)


def _dot_tn(a, b):
    return lax.dot_general(a, b, (((0,), (0,)), ((), ())), preferred_element_type=F32)


def _ple(h, p, gnorm, w_gate, w_proj):
    gate = jax.nn.sigmoid(_dot(_rms(h, gnorm).astype(BF16), w_gate))
    return h + gate * _dot(p.astype(BF16), w_proj)


def _ab_in_kernel(h_ref, g_ref, w_ref, e_ref, qg_ref, kg_ref, q_ref, k_ref, v_ref, u_ref):
    xn = _rms(h_ref[0], g_ref[...]).astype(BF16)
    a = _dot(xn, w_ref[...])
    e = e_ref[...]

    def head_norm(t, g):
        ss = _dot((t * t).astype(BF16), e)
        return t * lax.rsqrt(ss * (1.0 / A_HEAD_DIM) + EPS) * g

    q_ref[0, :, 0:256] = head_norm(a[:, 0:256], qg_ref[:, 0:256]).astype(BF16)
    q_ref[0, :, 256:512] = head_norm(a[:, 256:512], qg_ref[:, 256:512]).astype(BF16)
    k_ref[0] = head_norm(a[:, 512:768], kg_ref[...]).astype(BF16)
    v_ref[0] = a[:, 768:1024].astype(BF16)
    u_ref[0] = a[:, 1024:1536].astype(BF16)


def _ab_in(h, g, w, e, qg, kg, tm):
    B, S, D = h.shape
    tok = lambda w_: pl.BlockSpec((1, tm, w_), lambda b, i: (b, i, 0))
    return pl.pallas_call(
        _ab_in_kernel,
        grid=(B, S // tm),
        in_specs=[tok(D), _const_spec(g.shape), _const_spec(w.shape), _const_spec(e.shape),
                  _const_spec(qg.shape), _const_spec(kg.shape)],
        out_specs=[tok(512), tok(256), tok(256), tok(512)],
        out_shape=[jax.ShapeDtypeStruct((B, S, 512), BF16), jax.ShapeDtypeStruct((B, S, 256), BF16),
                   jax.ShapeDtypeStruct((B, S, 256), BF16), jax.ShapeDtypeStruct((B, S, 512), BF16)],
        compiler_params=_params(("parallel", "parallel")),
        name="ab_in",
    )(h, g, w, e, qg, kg)


def _win_pool_kernel(sink_ref, q_ref, kp_ref, km_ref, kn_ref, vp_ref, vm_ref, vn_ref,
                     up_ref, um_ref, un_ref, pw_ref, ps_ref, o_ref, kbuf, vbuf, ubuf, *, seq, tq):
    j = pl.program_id(1)
    nj = pl.num_programs(1)
    W = WIN_BLOCK
    kbuf[0:W] = kp_ref[0]
    kbuf[W:W + tq] = km_ref[0]
    kbuf[W + tq:] = kn_ref[0]
    vbuf[0:W] = vp_ref[0]
    vbuf[W:W + tq] = vm_ref[0]
    vbuf[W + tq:] = vn_ref[0]

    lane = lax.broadcasted_iota(jnp.int32, (1, 2 * LANES), 1) % LANES
    lo = lane < A_HEAD_DIM
    kb = kbuf[...]
    vb = vbuf[...]
    zero = jnp.zeros_like(kb)
    k_par = (jnp.where(lo, kb, zero), jnp.where(lo, zero, kb))
    v_par = (jnp.where(lo, vb, zero), jnp.where(lo, zero, vb))

    qi = lax.broadcasted_iota(jnp.int32, (W, 3 * W), 0)
    kj = lax.broadcasted_iota(jnp.int32, (W, 3 * W), 1)
    dist = jnp.abs(qi - kj + W)
    distf = dist.astype(F32)
    in_band = dist <= A_WINDOW
    G = A_HEADS // A_KV_HEADS
    for sb in range(tq // W):
        key_pos = j * tq + (sb - 1) * W + kj
        valid = in_band & (key_pos >= 0) & (key_pos < seq)
        neg = jnp.where(valid, 0.0, -jnp.inf)
        rows = slice(sb * W, (sb + 3) * W)
        for pair in range(A_HEADS // 2):
            g = (2 * pair) // G
            cols = slice(g * LANES, (g + 1) * LANES)
            qp = q_ref[0, sb * W:(sb + 1) * W, pair * LANES:(pair + 1) * LANES]
            o = jnp.zeros((W, LANES), F32)
            for par in range(2):
                h = 2 * pair + par
                slope = 2.0 ** (-8.0 * (h + 1) / A_HEADS)
                sink = sink_ref[h]
                s = _dot_nt(qp, k_par[par][rows, cols]) - slope * distf + neg
                m = jnp.maximum(jnp.max(s, axis=-1, keepdims=True), sink)
                ex = jnp.exp(s - m)
                den = jnp.sum(ex, axis=-1, keepdims=True) + jnp.exp(sink - m)
                pr = (ex / den).astype(BF16)
                o = o + _dot(pr, v_par[par][rows, cols])
            o_ref[0, sb * W:(sb + 1) * W, pair * LANES:(pair + 1) * LANES] = o.astype(BF16)

    H = POOL_HALO
    ubuf[0:H] = jnp.where(j > 0, up_ref[0].astype(F32), 0.0)
    ubuf[H:H + tq] = um_ref[0].astype(F32)
    ubuf[H + tq:] = jnp.where(j < nj - 1, un_ref[0].astype(F32), 0.0)
    t = j * tq + lax.broadcasted_iota(jnp.int32, (tq, 1), 0)
    for g, w in enumerate(B_POOL_SIZES):
        half = w // 2
        cols = slice(g * B_GROUP_DIM, (g + 1) * B_GROUP_DIM)
        tot = jnp.zeros((tq, B_GROUP_DIM), F32)
        for d in range(-half, half):
            tot = tot + ubuf[H + d:H + d + tq, cols]
        cnt = (jnp.minimum(t + half, seq) - jnp.maximum(t - half, 0)).astype(F32)
        dlt = (tot / cnt - ubuf[H:H + tq, cols]).astype(BF16)
        y = _dot(dlt, pw_ref[g]) * ps_ref[:, cols]
        o_ref[0, :, A_Q_DIM + g * B_GROUP_DIM:A_Q_DIM + (g + 1) * B_GROUP_DIM] = y.astype(BF16)


def _win_pool(q, k, v, u, sink, pool_w, pool_scale, tq):
    B, S, _ = q.shape
    W, H = WIN_BLOCK, POOL_HALO
    nw, nh = S // W, S // H
    rw, rh = tq // W, tq // H
    main = lambda w_: pl.BlockSpec((1, tq, w_), lambda b, j: (b, j, 0))
    prev = lambda rows, r, w_: pl.BlockSpec((1, rows, w_), lambda b, j: (b, jnp.maximum(j * r - 1, 0), 0))
    nxt = lambda rows, r, n, w_: pl.BlockSpec((1, rows, w_), lambda b, j: (b, jnp.minimum((j + 1) * r, n - 1), 0))
    kernel = functools.partial(_win_pool_kernel, seq=S, tq=tq)
    return pl.pallas_call(
        kernel,
        grid=(B, S // tq),
        in_specs=[pl.BlockSpec(memory_space=pltpu.SMEM), main(512),
                  prev(W, rw, 256), main(256), nxt(W, rw, nw, 256),
                  prev(W, rw, 256), main(256), nxt(W, rw, nw, 256),
                  prev(H, rh, 512), main(512), nxt(H, rh, nh, 512),
                  _const_spec(pool_w.shape), _const_spec(pool_scale.shape)],
        out_specs=main(1024),
        out_shape=jax.ShapeDtypeStruct((B, S, 1024), BF16),
        scratch_shapes=[pltpu.VMEM((tq + 2 * W, 256), BF16), pltpu.VMEM((tq + 2 * W, 256), BF16),
                        pltpu.VMEM((tq + 2 * H, 512), F32)],
        compiler_params=_params(("parallel", "parallel")),
        name="win_pool",
    )(sink, q, k, k, k, v, v, v, u, u, u, pool_w, pool_scale)


def _out_ffn_ple_kernel(h_ref, o_ref, p_ref, wo_ref, nf_ref, wg_ref, wu_ref, wd_ref,
                        gn_ref, pg_ref, pp_ref, out_ref, *, ff_chunk):
    h1 = h_ref[0] + _dot(o_ref[0], wo_ref[...])
    xn = _rms(h1, nf_ref[...]).astype(BF16)
    acc = jnp.zeros_like(h1)
    for c0 in range(0, D_FF, ff_chunk):
        g = _dot(xn, wg_ref[:, c0:c0 + ff_chunk])
        u = _dot(xn, wu_ref[:, c0:c0 + ff_chunk])
        hdn = (g * jax.nn.sigmoid(g) * u).astype(BF16)
        acc = acc + _dot(hdn, wd_ref[c0:c0 + ff_chunk, :])
    out_ref[0] = _ple(h1 + acc, p_ref[0], gn_ref[...], pg_ref[...], pp_ref[...])


def _out_ffn_ple(h, o, p, wo, nf, wg, wu, wd, gn, pg, pp, tm):
    B, S, D = h.shape
    tok = lambda w_: pl.BlockSpec((1, tm, w_), lambda b, i: (b, i, 0))
    consts = [wo, nf, wg, wu, wd, gn, pg, pp]
    return pl.pallas_call(
        functools.partial(_out_ffn_ple_kernel, ff_chunk=D_FF // 2),
        grid=(B, S // tm),
        in_specs=[tok(D), tok(D), tok(PLE_DIM)] + [_const_spec(c.shape) for c in consts],
        out_specs=tok(D),
        out_shape=jax.ShapeDtypeStruct((B, S, D), F32),
        compiler_params=_params(("parallel", "parallel")),
        name="out_ffn_ple",
    )(h, o, p, *consts)


def _mla_in_kernel(h_ref, g_ref, w_ref, qln_ref, kln_ref, wuq_ref, gq_ref, cosT_ref, sinT_ref,
                   wuk_ref, wuv_ref, gkn_ref, gkr_ref, cos_ref, sin_ref, rot_ref, place_ref, e_ref,
                   qT_ref, k_ref, vT_ref):
    xn = _rms(h_ref[0], g_ref[...]).astype(BF16)
    a = _dot(xn, w_ref[...])
    cq = _rms(a[:, 0:C_Q_LORA], qln_ref[...]).astype(BF16)
    ckv = _rms(a[:, C_Q_LORA:C_Q_LORA + C_KV_LORA], kln_ref[...]).astype(BF16)
    kpe = a[:, C_Q_LORA + C_KV_LORA:]

    qT = _dot_nt(wuq_ref[...], cq)
    cosT = cosT_ref[...]
    sinT = sinT_ref[...]
    half = C_ROPE // 2
    for h in range(C_HEADS):
        blk = qT[h * LANES:(h + 1) * LANES]
        ss = jnp.sum(blk * blk, axis=0, keepdims=True)
        qn = blk * lax.rsqrt(ss * (1.0 / C_QK) + EPS) * gq_ref[h * LANES:(h + 1) * LANES]
        x1 = qn[C_NOPE:C_NOPE + half]
        x2 = qn[C_NOPE + half:C_QK]
        out = jnp.concatenate([qn[0:C_NOPE], x1 * cosT - x2 * sinT, x1 * sinT + x2 * cosT, qn[C_QK:]], axis=0)
        qT_ref[0, h] = out.astype(BF16)

    vT = _dot_nt(wuv_ref[...], ckv)
    ones = jnp.ones((ATT_ONES, vT.shape[1]), BF16)
    for h in range(C_HEADS):
        vT_ref[0, h, 0:C_V] = vT[h * C_V:(h + 1) * C_V].astype(BF16)
        vT_ref[0, h, C_V:] = ones

    kraw = _dot(ckv, wuk_ref[...])
    kg = kpe * gkr_ref[...]
    krope = kg * cos_ref[...] + _dot(kg.astype(BF16), rot_ref[...]) * sin_ref[...]
    placed = _dot(krope.astype(BF16), place_ref[...])
    ss_pe = jnp.sum(kpe * kpe, axis=-1, keepdims=True)
    e = e_ref[...]
    for h2 in range(C_HEADS // 2):
        cols = slice(h2 * 2 * LANES, (h2 + 1) * 2 * LANES)
        kr = kraw[:, cols]
        ss = _dot((kr * kr).astype(BF16), e) + ss_pe
        kn = (kr * gkn_ref[:, cols] + placed[:, cols]) * lax.rsqrt(ss * (1.0 / C_QK) + EPS)
        k_ref[0, 2 * h2] = kn[:, 0:LANES].astype(BF16)
        k_ref[0, 2 * h2 + 1] = kn[:, LANES:].astype(BF16)


def _mla_in(h, g, w, qln, kln, wuq, gq, cosT, sinT, wuk, wuv, gkn, gkr, cos, sin, rot, place, e, tm):
    B, S, D = h.shape
    consts_a = [g, w, qln, kln, wuq, gq]
    consts_b = [wuk, wuv, gkn, gkr]
    consts_c = [rot, place, e]
    return pl.pallas_call(
        _mla_in_kernel,
        grid=(B, S // tm),
        in_specs=[pl.BlockSpec((1, tm, D), lambda b, i: (b, i, 0))]
        + [_const_spec(c.shape) for c in consts_a]
        + [pl.BlockSpec((C_ROPE // 2, tm), lambda b, i: (0, i))] * 2
        + [_const_spec(c.shape) for c in consts_b]
        + [pl.BlockSpec((tm, LANES), lambda b, i: (i, 0))] * 2
        + [_const_spec(c.shape) for c in consts_c],
        out_specs=[pl.BlockSpec((1, C_HEADS, LANES, tm), lambda b, i: (b, 0, 0, i)),
                   pl.BlockSpec((1, C_HEADS, tm, LANES), lambda b, i: (b, 0, i, 0)),
                   pl.BlockSpec((1, C_HEADS, C_V + ATT_ONES, tm), lambda b, i: (b, 0, 0, i))],
        out_shape=[jax.ShapeDtypeStruct((B, C_HEADS, LANES, S), BF16),
                   jax.ShapeDtypeStruct((B, C_HEADS, S, LANES), BF16),
                   jax.ShapeDtypeStruct((B, C_HEADS, C_V + ATT_ONES, S), BF16)],
        compiler_params=_params(("parallel", "parallel")),
        name="mla_in",
    )(h, *consts_a, cosT, sinT, *consts_b, cos, sin, *consts_c)


ATT_STRIP = 256
ATT_SUB = 128
ATT_ONES = 16
ATT_AHEAD = 12
ATT_TQ = 1024
ATT_TK = 1024


def _mla_attn_kernel(qT_ref, k_ref, vT_ref, oT_ref, m_sc, acc_sc, *, tk):
    seq = k_ref.shape[2]
    tq = qT_ref.shape[3]
    m_sc[...] = jnp.full_like(m_sc, -jnp.inf)
    acc_sc[...] = jnp.zeros_like(acc_sc)

    items = [(st, j) for j in range(tk // ATT_SUB) for st in range(tq // ATT_STRIP)]

    def body(c, carry):
        def key_off(j):
            return pl.multiple_of(c * tk + j * ATT_SUB, ATT_SUB)

        def scores(item):
            st, j = item
            return _dot(k_ref[0, 0, pl.ds(key_off(j), ATT_SUB), :],
                        qT_ref[0, 0, :, st * ATT_STRIP:(st + 1) * ATT_STRIP])

        pending = [scores(it) for it in items[:ATT_AHEAD]]
        for idx, (st, j) in enumerate(items):
            if idx + ATT_AHEAD < len(items):
                pending.append(scores(items[idx + ATT_AHEAD]))
            s = pending.pop(0)
            cols = slice(st * ATT_STRIP, (st + 1) * ATT_STRIP)
            m = m_sc[:, cols]
            m_new = jnp.maximum(m, jnp.max(s, axis=0, keepdims=True))
            p = jnp.exp2(s - m_new).astype(BF16)
            v1 = vT_ref[0, 0, :, pl.ds(key_off(j), ATT_SUB)]
            acc_sc[:, cols] = jnp.exp2(m - m_new) * acc_sc[:, cols] + _dot(v1, p)
            m_sc[:, cols] = m_new
        return carry

    lax.fori_loop(0, seq // tk, body, 0)
    acc = acc_sc[...]
    oT_ref[0, 0] = (acc[0:C_V] / acc[C_V:C_V + 1]).astype(BF16)


def _mla_attn(qT, k, vT, tq, tk):
    B, H, _, S = qT.shape
    return pl.pallas_call(
        functools.partial(_mla_attn_kernel, tk=tk),
        grid=(B, H, S // tq),
        in_specs=[pl.BlockSpec((1, 1, LANES, tq), lambda b, h, i: (b, h, 0, i)),
                  pl.BlockSpec((1, 1, S, LANES), lambda b, h, i: (b, h, 0, 0)),
                  pl.BlockSpec((1, 1, C_V + ATT_ONES, S), lambda b, h, i: (b, h, 0, 0))],
        out_specs=pl.BlockSpec((1, 1, C_V, tq), lambda b, h, i: (b, h, 0, i)),
        out_shape=jax.ShapeDtypeStruct((B, H, C_V, S), BF16),
        scratch_shapes=[pltpu.VMEM((1, tq), F32), pltpu.VMEM((C_V + ATT_ONES, tq), F32)],
        compiler_params=_params(("parallel", "parallel", "arbitrary")),
        name="mla_attn",
    )(qT, k, vT)


def _mla_out_router_kernel(h_ref, oT_ref, wo_ref, nf_ref, r_ref, h1_ref, xn_ref, route_ref):
    h1 = h_ref[0] + _dot_tn(oT_ref[0], wo_ref[...])
    h1_ref[0] = h1
    xn = _rms(h1, nf_ref[...])
    hi = xn.astype(BF16)
    xn_ref[0] = hi
    lo = (xn - hi.astype(F32)).astype(BF16)
    p_hi = _dot(hi, r_ref[...])
    p_lo = _dot(lo, r_ref[...])
    logits = p_hi + p_lo + pltpu.roll(p_hi, LANES - MOE_EXPERTS, axis=1)
    lane = lax.broadcasted_iota(jnp.int32, logits.shape, 1)
    logits = jnp.where(lane < MOE_EXPERTS, logits, -jnp.inf)
    m1 = jnp.max(logits, axis=-1, keepdims=True)
    i1 = jnp.min(jnp.where(logits == m1, lane, LANES), axis=-1, keepdims=True)
    rest = jnp.where(lane == i1, -jnp.inf, logits)
    m2 = jnp.max(rest, axis=-1, keepdims=True)
    i2 = jnp.min(jnp.where(rest == m2, lane, LANES), axis=-1, keepdims=True)
    e2 = jnp.exp(m2 - m1)
    w1 = 1.0 / (1.0 + e2)
    w2 = e2 * w1
    route = jnp.where(lane == 0, i1.astype(F32),
                      jnp.where(lane == 1, i2.astype(F32),
                                jnp.where(lane == 2, w1, jnp.where(lane == 3, w2, 0.0))))
    route_ref[0] = route


def _mla_out_router(h, oT, wo, nf, r, tm):
    B, S, D = h.shape
    tok = lambda w_: pl.BlockSpec((1, tm, w_), lambda b, i: (b, i, 0))
    return pl.pallas_call(
        _mla_out_router_kernel,
        grid=(B, S // tm),
        in_specs=[tok(D), pl.BlockSpec((1, D, tm), lambda b, i: (b, 0, i)),
                  _const_spec(wo.shape), _const_spec(nf.shape), _const_spec(r.shape)],
        out_specs=[tok(D), tok(D), tok(LANES)],
        out_shape=[jax.ShapeDtypeStruct((B, S, D), F32), jax.ShapeDtypeStruct((B, S, D), BF16),
                   jax.ShapeDtypeStruct((B, S, LANES), F32)],
        compiler_params=_params(("parallel", "parallel")),
        name="mla_out_router",
    )(h, oT, wo, nf, r)


def _experts_kernel(te_ref, tv_ref, x_ref, wg_ref, wu_ref, wd_ref, y_ref, acc_sc):
    i = pl.program_id(0)
    f = pl.program_id(1)

    @pl.when(tv_ref[i] > 0)
    def _():
        x = x_ref[...]
        g = _dot(x, wg_ref[0])
        u = _dot(x, wu_ref[0])
        hdn = (g * jax.nn.sigmoid(g) * u).astype(BF16)
        y = _dot(hdn, wd_ref[0])

        @pl.when(f == 0)
        def _():
            acc_sc[...] = y

        @pl.when(f == pl.num_programs(1) - 1)
        def _():
            y_ref[...] = (acc_sc[...] + y).astype(BF16)


def _experts(xs, te, tv, wg, wu, wd, tm, nf):
    P, D = xs.shape
    fc = E_FF // nf
    grid_spec = pltpu.PrefetchScalarGridSpec(
        num_scalar_prefetch=2,
        grid=(P // tm, nf),
        in_specs=[pl.BlockSpec((tm, D), lambda i, f, te, tv: (i, 0)),
                  pl.BlockSpec((1, D, fc), lambda i, f, te, tv: (te[i], 0, f)),
                  pl.BlockSpec((1, D, fc), lambda i, f, te, tv: (te[i], 0, f)),
                  pl.BlockSpec((1, fc, D), lambda i, f, te, tv: (te[i], f, 0))],
        out_specs=pl.BlockSpec((tm, D), lambda i, f, te, tv: (i, 0)),
        scratch_shapes=[pltpu.VMEM((tm, D), F32)],
    )
    return pl.pallas_call(
        _experts_kernel,
        grid_spec=grid_spec,
        out_shape=jax.ShapeDtypeStruct((P, D), BF16),
        compiler_params=_params(("parallel", "arbitrary")),
        name="experts",
    )(te, tv, xs, wg, wu, wd)


def _combine_ple_kernel(h_ref, y0_ref, y1_ref, r_ref, p_ref, gn_ref, pg_ref, pp_ref, out_ref):
    r = r_ref[0]
    h2 = h_ref[0] + r[:, 2:3] * y0_ref[0].astype(F32) + r[:, 3:4] * y1_ref[0].astype(F32)
    out_ref[0] = _ple(h2, p_ref[0], gn_ref[...], pg_ref[...], pp_ref[...])


def _combine_ple(h, y0, y1, route, p, gn, pg, pp, tm):
    B, S, D = h.shape
    tok = lambda w_: pl.BlockSpec((1, tm, w_), lambda b, i: (b, i, 0))
    consts = [gn, pg, pp]
    return pl.pallas_call(
        _combine_ple_kernel,
        grid=(B, S // tm),
        in_specs=[tok(D), tok(D), tok(D), tok(LANES), tok(PLE_DIM)] + [_const_spec(c.shape) for c in consts],
        out_specs=tok(D),
        out_shape=jax.ShapeDtypeStruct((B, S, D), F32),
        compiler_params=_params(("parallel", "parallel")),
        name="combine_ple",
    )(h, y0, y1, route, p, *consts)


def _block_diag_ones(n, blk):
    i = np.arange(n)
    return jnp.asarray((i[:, None] // blk == i[None, :] // blk).astype(np.float32), BF16)


def _prep_even(W, j, i):
    w = W['ab_w_in'][j]
    c = lambda a, b: w[:, a:b]
    k0, k1 = c(512, 576), c(576, 640)
    v0, v1 = c(640, 704), c(704, 768)
    w_in = jnp.concatenate([c(0, 512), k0, k0, k1, k1, v0, v0, v1, v1, c(768, 1280)], axis=1).astype(BF16)
    return dict(
        g=W['norm_mix'][i][None], w_in=w_in, e=_block_diag_ones(256, A_HEAD_DIM),
        qg=(jnp.tile(W['ab_q_norm'][j], A_HEADS) * A_HEAD_DIM ** -0.5)[None],
        kg=jnp.tile(W['ab_k_norm'][j], 2 * A_KV_HEADS)[None],
        sink=W['ab_sink'][j], pool_w=W['ab_pool_w'][j].astype(BF16), pool_scale=W['ab_pool_scale'][j][None],
        wo=W['ab_w_out'][j].astype(BF16), nf=W['norm_ffn'][i][None],
        wg=W['ffn_w_gate'][j].astype(BF16), wu=W['ffn_w_up'][j].astype(BF16), wd=W['ffn_w_down'][j].astype(BF16),
        gn=W['ple_gate_norm'][i][None], pg=W['ple_w_gate'][i].astype(BF16), pp=W['ple_w_proj'][i].astype(BF16))


def _prep_odd(W, j, i):
    pad_h = lambda a, n: jnp.pad(a, [(0, 0)] * (a.ndim - 1) + [(0, n - a.shape[-1])])
    w_in = pad_h(W['mla_w_in'][j], 512).astype(BF16)
    wuq = pad_h(W['mla_w_uq'][j].reshape(C_Q_LORA, C_HEADS, C_QK), LANES).reshape(C_Q_LORA, C_HEADS * LANES)
    gq = jnp.tile(pad_h(W['mla_q_norm'][j], LANES), C_HEADS) * (C_QK ** -0.5 * math.log2(math.e))
    ukv = W['mla_w_ukv'][j].reshape(C_KV_LORA, C_HEADS, C_NOPE + C_V)
    wuk = pad_h(ukv[..., :C_NOPE], LANES).reshape(C_KV_LORA, C_HEADS * LANES)
    wuv = ukv[..., C_NOPE:].reshape(C_KV_LORA, C_HEADS * C_V)
    kn = W['mla_k_norm'][j]
    gkn = jnp.tile(pad_h(kn[:C_NOPE], LANES), C_HEADS)[None]
    gkr = pad_h(kn[C_NOPE:], LANES)[None]
    half = C_ROPE // 2
    rot = np.zeros((LANES, LANES), np.float32)
    rot[np.arange(half) + half, np.arange(half)] = -1.0
    rot[np.arange(half), np.arange(half) + half] = 1.0
    place = np.zeros((LANES, C_HEADS * LANES), np.float32)
    for h in range(C_HEADS):
        place[np.arange(C_ROPE), h * LANES + C_NOPE + np.arange(C_ROPE)] = 1.0
    router = W['moe_router'][j]
    r_hi = router.astype(BF16)
    r_lo = (router - r_hi.astype(F32)).astype(BF16)
    r = pad_h(jnp.concatenate([r_hi, r_lo], axis=1), LANES)
    return dict(
        g=W['norm_mix'][i][None], w_in=w_in, qln=W['mla_q_lat_norm'][j][None], kln=W['mla_kv_lat_norm'][j][None],
        wuq=wuq.T.astype(BF16), gq=gq[:, None], wuk=wuk.astype(BF16), wuv=wuv.T.astype(BF16), gkn=gkn, gkr=gkr,
        rot=jnp.asarray(rot, BF16), place=jnp.asarray(place, BF16), e=_block_diag_ones(2 * LANES, LANES),
        wo=W['mla_w_out'][j].astype(BF16), nf=W['norm_ffn'][i][None], r=r,
        wg=W['moe_w_gate'][j].astype(BF16), wu=W['moe_w_up'][j].astype(BF16), wd=W['moe_w_down'][j].astype(BF16),
        gn=W['ple_gate_norm'][i][None], pg=W['ple_w_gate'][i].astype(BF16), pp=W['ple_w_proj'][i].astype(BF16))


def _rope_tables(S):
    half = C_ROPE // 2
    inv = ROPE_THETA ** (-jnp.arange(half, dtype=F32) / half)
    ang = jnp.arange(S).astype(F32)[:, None] * inv[None, :]
    cos, sin = jnp.cos(ang), jnp.sin(ang)
    wide = lambda t: jnp.pad(jnp.concatenate([t, t], axis=1), ((0, 0), (0, LANES - C_ROPE)))
    return cos.T, sin.T, wide(cos), wide(sin)


def _route_layout(ids, tm):
    n_slots = ids.size
    n_tiles = n_slots // tm + MOE_EXPERTS
    flat = ids.reshape(-1)
    order = jnp.argsort(flat, stable=True)
    counts = jnp.sum(flat[:, None] == jnp.arange(MOE_EXPERTS)[None, :], axis=0)
    tiles = (counts + tm - 1) // tm
    tile_start = jnp.cumsum(tiles) - tiles
    slot_start = jnp.cumsum(counts) - counts
    e_sorted = flat[order]
    row_sorted = tile_start[e_sorted] * tm + jnp.arange(n_slots) - slot_start[e_sorted]
    src = jnp.zeros((n_tiles * tm,), jnp.int32).at[row_sorted].set((order // 2).astype(jnp.int32))
    row_of_slot = jnp.zeros((n_slots,), jnp.int32).at[order].set(row_sorted.astype(jnp.int32))
    tile_idx = jnp.arange(n_tiles)
    te = jnp.sum(tile_idx[:, None] >= jnp.cumsum(tiles)[None, :], axis=1)
    tv = (te < MOE_EXPERTS).astype(jnp.int32)
    te = jnp.minimum(te, MOE_EXPERTS - 1).astype(jnp.int32)
    return src, te, tv, row_of_slot.reshape(ids.shape)


def _tiles(S):
    return dict(tm=min(512, S), tq_win=min(512, S), tq=min(ATT_TQ, S), tk=min(ATT_TK, S), te=512, nf=2)


def _trunk(x, p, even, odd):
    B, S, D = x.shape
    t = _tiles(S)
    q, k, v, u = _ab_in(x, even['g'], even['w_in'], even['e'], even['qg'], even['kg'], t['tm'])
    o = _win_pool(q, k, v, u, even['sink'], even['pool_w'], even['pool_scale'], t['tq_win'])
    h = _out_ffn_ple(x, o, p[0], even['wo'], even['nf'], even['wg'], even['wu'], even['wd'],
                     even['gn'], even['pg'], even['pp'], t['tm'])
    cosT, sinT, cos, sin = _rope_tables(S)
    qT, kk, vT = _mla_in(h, odd['g'], odd['w_in'], odd['qln'], odd['kln'], odd['wuq'], odd['gq'], cosT, sinT,
                         odd['wuk'], odd['wuv'], odd['gkn'], odd['gkr'], cos, sin, odd['rot'], odd['place'],
                         odd['e'], t['tm'])
    oT = _mla_attn(qT, kk, vT, t['tq'], t['tk'])
    h1, xn, route = _mla_out_router(h, oT.reshape(B, C_HEADS * C_V, S), odd['wo'], odd['nf'], odd['r'], t['tm'])
    ids = route[..., 0:2].astype(jnp.int32).reshape(B * S, 2)
    src, te, tv, rows = _route_layout(ids, t['te'])
    xs = jnp.take(xn.reshape(B * S, D), src, axis=0)
    ys = _experts(xs, te, tv, odd['wg'], odd['wu'], odd['wd'], t['te'], t['nf'])
    y0 = jnp.take(ys, rows[:, 0], axis=0).reshape(B, S, D)
    y1 = jnp.take(ys, rows[:, 1], axis=0).reshape(B, S, D)
    return _combine_ple(h1, y0, y1, route, p[1], odd['gn'], odd['pg'], odd['pp'], t['tm'])


def kernel(x_prompt, x_sample, p_prompt, p_sample, norm_mix, norm_ffn, ab_w_in, ab_q_norm, ab_k_norm, ab_sink, ab_pool_w, ab_pool_scale, ab_w_out, ffn_w_gate, ffn_w_up, ffn_w_down, mla_w_in, mla_q_lat_norm, mla_kv_lat_norm, mla_w_uq, mla_w_ukv, mla_q_norm, mla_k_norm, mla_w_out, moe_router, moe_w_gate, moe_w_up, moe_w_down, ple_w_proj, ple_gate_norm, ple_w_gate):
    W = dict(norm_mix=norm_mix, norm_ffn=norm_ffn,
             ab_w_in=ab_w_in, ab_q_norm=ab_q_norm, ab_k_norm=ab_k_norm, ab_sink=ab_sink,
             ab_pool_w=ab_pool_w, ab_pool_scale=ab_pool_scale, ab_w_out=ab_w_out,
             ffn_w_gate=ffn_w_gate, ffn_w_up=ffn_w_up, ffn_w_down=ffn_w_down,
             mla_w_in=mla_w_in, mla_q_lat_norm=mla_q_lat_norm, mla_kv_lat_norm=mla_kv_lat_norm,
             mla_w_uq=mla_w_uq, mla_w_ukv=mla_w_ukv, mla_q_norm=mla_q_norm, mla_k_norm=mla_k_norm,
             mla_w_out=mla_w_out, moe_router=moe_router, moe_w_gate=moe_w_gate, moe_w_up=moe_w_up,
             moe_w_down=moe_w_down, ple_w_proj=ple_w_proj, ple_gate_norm=ple_gate_norm,
             ple_w_gate=ple_w_gate)
    even = _prep_even(W, 0, 0)
    odd = _prep_odd(W, 0, 1)
    return (_trunk(x_prompt, p_prompt, even, odd), _trunk(x_sample, p_sample, even, odd))
```

```python
import functools
import math

import numpy as np
import jax
import jax.numpy as jnp
from jax import lax
from jax.experimental import pallas as pl
from jax.experimental.pallas import tpu as pltpu

F32 = jnp.float32
BF16 = jnp.bfloat16

D_MODEL = 1024
EPS = 1e-6
LANES = 128
VMEM_LIMIT = 56 * 1024 * 1024

A_HEADS, A_KV_HEADS, A_HEAD_DIM, A_WINDOW = 8, 2, 64, 128
A_Q_DIM = A_HEADS * A_HEAD_DIM
B_WIDTH, B_GROUPS, B_GROUP_DIM = 512, 4, 128
B_POOL_SIZES = (2, 4, 8, 16)
POOL_HALO = 16
WIN_BLOCK = 128
C_HEADS, C_NOPE, C_ROPE, C_V = 16, 64, 32, 64
C_QK = C_NOPE + C_ROPE
C_Q_LORA, C_KV_LORA = 256, 128
ROPE_THETA = 10000.0
D_FF, MOE_EXPERTS, E_FF = 2816, 8, 3584
PLE_DIM = 256


def _const_spec(shape):
    nd = len(shape)
    return pl.BlockSpec(shape, lambda *_: (0,) * nd, pipeline_mode=pl.Buffered(1))


def _params(sem):
    return pltpu.CompilerParams(dimension_semantics=sem, vmem_limit_bytes=VMEM_LIMIT)


def _rms(x, g):
    ms = jnp.mean(x * x, axis=-1, keepdims=True)
    return x * lax.rsqrt(ms + EPS) * g


def _dot(a, b):
    return jnp.dot(a, b, preferred_element_type=F32)


def _dot_nt(a, b):
    return lax.dot_general(a, b, (((1,), (1,)), ((), ())), preferred_element_type=F32)


def _dot_tn(a, b):
    return lax.dot_general(a, b, (((0,), (0,)), ((), ())), preferred_element_type=F32)


def _ple(h, p, gnorm, w_gate, w_proj):
    gate = jax.nn.sigmoid(_dot(_rms(h, gnorm).astype(BF16), w_gate))
    return h + gate * _dot(p.astype(BF16), w_proj)


def _ab_in_kernel(h_ref, g_ref, w_ref, e_ref, qg_ref, kg_ref, q_ref, k_ref, v_ref, u_ref):
    xn = _rms(h_ref[0], g_ref[...]).astype(BF16)
    a = _dot(xn, w_ref[...])
    e = e_ref[...]

    def head_norm(t, g):
        ss = _dot((t * t).astype(BF16), e)
        return t * lax.rsqrt(ss * (1.0 / A_HEAD_DIM) + EPS) * g

    q_ref[0, :, 0:256] = head_norm(a[:, 0:256], qg_ref[:, 0:256]).astype(BF16)
    q_ref[0, :, 256:512] = head_norm(a[:, 256:512], qg_ref[:, 256:512]).astype(BF16)
    k_ref[0] = head_norm(a[:, 512:768], kg_ref[...]).astype(BF16)
    v_ref[0] = a[:, 768:1024].astype(BF16)
    u_ref[0] = a[:, 1024:1536].astype(BF16)


def _ab_in(h, g, w, e, qg, kg, tm):
    B, S, D = h.shape
    tok = lambda w_: pl.BlockSpec((1, tm, w_), lambda b, i: (b, i, 0))
    return pl.pallas_call(
        _ab_in_kernel,
        grid=(B, S // tm),
        in_specs=[tok(D), _const_spec(g.shape), _const_spec(w.shape), _const_spec(e.shape),
                  _const_spec(qg.shape), _const_spec(kg.shape)],
        out_specs=[tok(512), tok(256), tok(256), tok(512)],
        out_shape=[jax.ShapeDtypeStruct((B, S, 512), BF16), jax.ShapeDtypeStruct((B, S, 256), BF16),
                   jax.ShapeDtypeStruct((B, S, 256), BF16), jax.ShapeDtypeStruct((B, S, 512), BF16)],
        compiler_params=_params(("parallel", "parallel")),
        name="ab_in",
    )(h, g, w, e, qg, kg)


def _win_pool_kernel(sink_ref, q_ref, kp_ref, km_ref, kn_ref, vp_ref, vm_ref, vn_ref,
                     up_ref, um_ref, un_ref, pw_ref, ps_ref, o_ref, kbuf, vbuf, ubuf, *, seq, tq):
    j = pl.program_id(1)
    nj = pl.num_programs(1)
    W = WIN_BLOCK
    kbuf[0:W] = kp_ref[0]
    kbuf[W:W + tq] = km_ref[0]
    kbuf[W + tq:] = kn_ref[0]
    vbuf[0:W] = vp_ref[0]
    vbuf[W:W + tq] = vm_ref[0]
    vbuf[W + tq:] = vn_ref[0]

    lane = lax.broadcasted_iota(jnp.int32, (1, 2 * LANES), 1) % LANES
    lo = lane < A_HEAD_DIM
    kb = kbuf[...]
    vb = vbuf[...]
    zero = jnp.zeros_like(kb)
    k_par = (jnp.where(lo, kb, zero), jnp.where(lo, zero, kb))
    v_par = (jnp.where(lo, vb, zero), jnp.where(lo, zero, vb))

    qi = lax.broadcasted_iota(jnp.int32, (W, 3 * W), 0)
    kj = lax.broadcasted_iota(jnp.int32, (W, 3 * W), 1)
    dist = jnp.abs(qi - kj + W)
    distf = dist.astype(F32)
    in_band = dist <= A_WINDOW
    G = A_HEADS // A_KV_HEADS
    for sb in range(tq // W):
        key_pos = j * tq + (sb - 1) * W + kj
        valid = in_band & (key_pos >= 0) & (key_pos < seq)
        neg = jnp.where(valid, 0.0, -jnp.inf)
        rows = slice(sb * W, (sb + 3) * W)
        for pair in range(A_HEADS // 2):
            g = (2 * pair) // G
            cols = slice(g * LANES, (g + 1) * LANES)
            qp = q_ref[0, sb * W:(sb + 1) * W, pair * LANES:(pair + 1) * LANES]
            o = jnp.zeros((W, LANES), F32)
            for par in range(2):
                h = 2 * pair + par
                slope = 2.0 ** (-8.0 * (h + 1) / A_HEADS)
                sink = sink_ref[h]
                s = _dot_nt(qp, k_par[par][rows, cols]) - slope * distf + neg
                m = jnp.maximum(jnp.max(s, axis=-1, keepdims=True), sink)
                ex = jnp.exp(s - m)
                den = jnp.sum(ex, axis=-1, keepdims=True) + jnp.exp(sink - m)
                pr = (ex / den).astype(BF16)
                o = o + _dot(pr, v_par[par][rows, cols])
            o_ref[0, sb * W:(sb + 1) * W, pair * LANES:(pair + 1) * LANES] = o.astype(BF16)

    H = POOL_HALO
    ubuf[0:H] = jnp.where(j > 0, up_ref[0].astype(F32), 0.0)
    ubuf[H:H + tq] = um_ref[0].astype(F32)
    ubuf[H + tq:] = jnp.where(j < nj - 1, un_ref[0].astype(F32), 0.0)
    t = j * tq + lax.broadcasted_iota(jnp.int32, (tq, 1), 0)
    for g, w in enumerate(B_POOL_SIZES):
        half = w // 2
        cols = slice(g * B_GROUP_DIM, (g + 1) * B_GROUP_DIM)
        tot = jnp.zeros((tq, B_GROUP_DIM), F32)
        for d in range(-half, half):
            tot = tot + ubuf[H + d:H + d + tq, cols]
        cnt = (jnp.minimum(t + half, seq) - jnp.maximum(t - half, 0)).astype(F32)
        dlt = (tot / cnt - ubuf[H:H + tq, cols]).astype(BF16)
        y = _dot(dlt, pw_ref[g]) * ps_ref[:, cols]
        o_ref[0, :, A_Q_DIM + g * B_GROUP_DIM:A_Q_DIM + (g + 1) * B_GROUP_DIM] = y.astype(BF16)


def _win_pool(q, k, v, u, sink, pool_w, pool_scale, tq):
    B, S, _ = q.shape
    W, H = WIN_BLOCK, POOL_HALO
    nw, nh = S // W, S // H
    rw, rh = tq // W, tq // H
    main = lambda w_: pl.BlockSpec((1, tq, w_), lambda b, j: (b, j, 0))
    prev = lambda rows, r, w_: pl.BlockSpec((1, rows, w_), lambda b, j: (b, jnp.maximum(j * r - 1, 0), 0))
    nxt = lambda rows, r, n, w_: pl.BlockSpec((1, rows, w_), lambda b, j: (b, jnp.minimum((j + 1) * r, n - 1), 0))
    kernel = functools.partial(_win_pool_kernel, seq=S, tq=tq)
    return pl.pallas_call(
        kernel,
        grid=(B, S // tq),
        in_specs=[pl.BlockSpec(memory_space=pltpu.SMEM), main(512),
                  prev(W, rw, 256), main(256), nxt(W, rw, nw, 256),
                  prev(W, rw, 256), main(256), nxt(W, rw, nw, 256),
                  prev(H, rh, 512), main(512), nxt(H, rh, nh, 512),
                  _const_spec(pool_w.shape), _const_spec(pool_scale.shape)],
        out_specs=main(1024),
        out_shape=jax.ShapeDtypeStruct((B, S, 1024), BF16),
        scratch_shapes=[pltpu.VMEM((tq + 2 * W, 256), BF16), pltpu.VMEM((tq + 2 * W, 256), BF16),
                        pltpu.VMEM((tq + 2 * H, 512), F32)],
        compiler_params=_params(("parallel", "parallel")),
        name="win_pool",
    )(sink, q, k, k, k, v, v, v, u, u, u, pool_w, pool_scale)


def _out_ffn_ple_kernel(h_ref, o_ref, p_ref, wo_ref, nf_ref, wg_ref, wu_ref, wd_ref,
                        gn_ref, pg_ref, pp_ref, out_ref, *, ff_chunk):
    h1 = h_ref[0] + _dot(o_ref[0], wo_ref[...])
    xn = _rms(h1, nf_ref[...]).astype(BF16)
    acc = jnp.zeros_like(h1)
    for c0 in range(0, D_FF, ff_chunk):
        g = _dot(xn, wg_ref[:, c0:c0 + ff_chunk])
        u = _dot(xn, wu_ref[:, c0:c0 + ff_chunk])
        hdn = (g * jax.nn.sigmoid(g) * u).astype(BF16)
        acc = acc + _dot(hdn, wd_ref[c0:c0 + ff_chunk, :])
    out_ref[0] = _ple(h1 + acc, p_ref[0], gn_ref[...], pg_ref[...], pp_ref[...])


def _out_ffn_ple(h, o, p, wo, nf, wg, wu, wd, gn, pg, pp, tm):
    B, S, D = h.shape
    tok = lambda w_: pl.BlockSpec((1, tm, w_), lambda b, i: (b, i, 0))
    consts = [wo, nf, wg, wu, wd, gn, pg, pp]
    return pl.pallas_call(
        functools.partial(_out_ffn_ple_kernel, ff_chunk=D_FF // 2),
        grid=(B, S // tm),
        in_specs=[tok(D), tok(D), tok(PLE_DIM)] + [_const_spec(c.shape) for c in consts],
        out_specs=tok(D),
        out_shape=jax.ShapeDtypeStruct((B, S, D), F32),
        compiler_params=_params(("parallel", "parallel")),
        name="out_ffn_ple",
    )(h, o, p, *consts)


def _mla_in_kernel(h_ref, g_ref, w_ref, qln_ref, kln_ref, wuq_ref, gq_ref, cosT_ref, sinT_ref,
                   wuk_ref, wuv_ref, gkn_ref, gkr_ref, cos_ref, sin_ref, rot_ref, place_ref, e_ref,
                   qT_ref, k_ref, vT_ref):
    xn = _rms(h_ref[0], g_ref[...]).astype(BF16)
    a = _dot(xn, w_ref[...])
    cq = _rms(a[:, 0:C_Q_LORA], qln_ref[...]).astype(BF16)
    ckv = _rms(a[:, C_Q_LORA:C_Q_LORA + C_KV_LORA], kln_ref[...]).astype(BF16)
    kpe = a[:, C_Q_LORA + C_KV_LORA:]

    qT = _dot_nt(wuq_ref[...], cq)
    cosT = cosT_ref[...]
    sinT = sinT_ref[...]
    half = C_ROPE // 2
    for h in range(C_HEADS):
        blk = qT[h * LANES:(h + 1) * LANES]
        ss = jnp.sum(blk * blk, axis=0, keepdims=True)
        qn = blk * lax.rsqrt(ss * (1.0 / C_QK) + EPS) * gq_ref[h * LANES:(h + 1) * LANES]
        x1 = qn[C_NOPE:C_NOPE + half]
        x2 = qn[C_NOPE + half:C_QK]
        out = jnp.concatenate([qn[0:C_NOPE], x1 * cosT - x2 * sinT, x1 * sinT + x2 * cosT, qn[C_QK:]], axis=0)
        qT_ref[0, h] = out.astype(BF16)

    vT = _dot_nt(wuv_ref[...], ckv)
    ones = jnp.ones((ATT_ONES, vT.shape[1]), BF16)
    for h in range(C_HEADS):
        vT_ref[0, h, 0:C_V] = vT[h * C_V:(h + 1) * C_V].astype(BF16)
        vT_ref[0, h, C_V:] = ones

    kraw = _dot(ckv, wuk_ref[...])
    kg = kpe * gkr_ref[...]
    krope = kg * cos_ref[...] + _dot(kg.astype(BF16), rot_ref[...]) * sin_ref[...]
    placed = _dot(krope.astype(BF16), place_ref[...])
    ss_pe = jnp.sum(kpe * kpe, axis=-1, keepdims=True)
    e = e_ref[...]
    for h2 in range(C_HEADS // 2):
        cols = slice(h2 * 2 * LANES, (h2 + 1) * 2 * LANES)
        kr = kraw[:, cols]
        ss = _dot((kr * kr).astype(BF16), e) + ss_pe
        kn = (kr * gkn_ref[:, cols] + placed[:, cols]) * lax.rsqrt(ss * (1.0 / C_QK) + EPS)
        k_ref[0, 2 * h2] = kn[:, 0:LANES].astype(BF16)
        k_ref[0, 2 * h2 + 1] = kn[:, LANES:].astype(BF16)


def _mla_in(h, g, w, qln, kln, wuq, gq, cosT, sinT, wuk, wuv, gkn, gkr, cos, sin, rot, place, e, tm):
    B, S, D = h.shape
    consts_a = [g, w, qln, kln, wuq, gq]
    consts_b = [wuk, wuv, gkn, gkr]
    consts_c = [rot, place, e]
    return pl.pallas_call(
        _mla_in_kernel,
        grid=(B, S // tm),
        in_specs=[pl.BlockSpec((1, tm, D), lambda b, i: (b, i, 0))]
        + [_const_spec(c.shape) for c in consts_a]
        + [pl.BlockSpec((C_ROPE // 2, tm), lambda b, i: (0, i))] * 2
        + [_const_spec(c.shape) for c in consts_b]
        + [pl.BlockSpec((tm, LANES), lambda b, i: (i, 0))] * 2
        + [_const_spec(c.shape) for c in consts_c],
        out_specs=[pl.BlockSpec((1, C_HEADS, LANES, tm), lambda b, i: (b, 0, 0, i)),
                   pl.BlockSpec((1, C_HEADS, tm, LANES), lambda b, i: (b, 0, i, 0)),
                   pl.BlockSpec((1, C_HEADS, C_V + ATT_ONES, tm), lambda b, i: (b, 0, 0, i))],
        out_shape=[jax.ShapeDtypeStruct((B, C_HEADS, LANES, S), BF16),
                   jax.ShapeDtypeStruct((B, C_HEADS, S, LANES), BF16),
                   jax.ShapeDtypeStruct((B, C_HEADS, C_V + ATT_ONES, S), BF16)],
        compiler_params=_params(("parallel", "parallel")),
        name="mla_in",
    )(h, *consts_a, cosT, sinT, *consts_b, cos, sin, *consts_c)


ATT_STRIP = 256
ATT_SUB = 128
ATT_ONES = 16
ATT_AHEAD = 12
ATT_TQ = 1024
ATT_TK = 2048


def _mla_attn_kernel(qT_ref, k_ref, vT_ref, oT_ref, m_sc, acc_sc, *, tk):
    seq = k_ref.shape[2]
    tq = qT_ref.shape[3]
    m_sc[...] = jnp.full_like(m_sc, -jnp.inf)
    acc_sc[...] = jnp.zeros_like(acc_sc)

    items = [(st, j) for j in range(tk // ATT_SUB) for st in range(tq // ATT_STRIP)]

    def body(c, carry):
        def key_off(j):
            return pl.multiple_of(c * tk + j * ATT_SUB, ATT_SUB)

        def scores(item):
            st, j = item
            return _dot(k_ref[0, 0, pl.ds(key_off(j), ATT_SUB), :],
                        qT_ref[0, 0, :, st * ATT_STRIP:(st + 1) * ATT_STRIP])

        pending = [scores(it) for it in items[:ATT_AHEAD]]
        for idx, (st, j) in enumerate(items):
            if idx + ATT_AHEAD < len(items):
                pending.append(scores(items[idx + ATT_AHEAD]))
            s = pending.pop(0)
            cols = slice(st * ATT_STRIP, (st + 1) * ATT_STRIP)
            m = m_sc[:, cols]
            m_new = jnp.maximum(m, jnp.max(s, axis=0, keepdims=True))
            p = jnp.exp2(s - m_new).astype(BF16)
            v1 = vT_ref[0, 0, :, pl.ds(key_off(j), ATT_SUB)]
            acc_sc[:, cols] = jnp.exp2(m - m_new) * acc_sc[:, cols] + _dot(v1, p)
            m_sc[:, cols] = m_new
        return carry

    lax.fori_loop(0, seq // tk, body, 0)
    acc = acc_sc[...]
    oT_ref[0, 0] = (acc[0:C_V] / acc[C_V:C_V + 1]).astype(BF16)


def _mla_attn(qT, k, vT, tq, tk):
    B, H, _, S = qT.shape
    return pl.pallas_call(
        functools.partial(_mla_attn_kernel, tk=tk),
        grid=(B, H, S // tq),
        in_specs=[pl.BlockSpec((1, 1, LANES, tq), lambda b, h, i: (b, h, 0, i)),
                  pl.BlockSpec((1, 1, S, LANES), lambda b, h, i: (b, h, 0, 0)),
                  pl.BlockSpec((1, 1, C_V + ATT_ONES, S), lambda b, h, i: (b, h, 0, 0))],
        out_specs=pl.BlockSpec((1, 1, C_V, tq), lambda b, h, i: (b, h, 0, i)),
        out_shape=jax.ShapeDtypeStruct((B, H, C_V, S), BF16),
        scratch_shapes=[pltpu.VMEM((1, tq), F32), pltpu.VMEM((C_V + ATT_ONES, tq), F32)],
        compiler_params=_params(("parallel", "parallel", "arbitrary")),
        name="mla_attn",
    )(qT, k, vT)


def _mla_out_router_kernel(h_ref, oT_ref, wo_ref, nf_ref, r_ref, h1_ref, xn_ref, route_ref):
    h1 = h_ref[0] + _dot_tn(oT_ref[0], wo_ref[...])
    h1_ref[0] = h1
    xn = _rms(h1, nf_ref[...])
    hi = xn.astype(BF16)
    xn_ref[0] = hi
    lo = (xn - hi.astype(F32)).astype(BF16)
    p_hi = _dot(hi, r_ref[...])
    p_lo = _dot(lo, r_ref[...])
    logits = p_hi + p_lo + pltpu.roll(p_hi, LANES - MOE_EXPERTS, axis=1)
    lane = lax.broadcasted_iota(jnp.int32, logits.shape, 1)
    logits = jnp.where(lane < MOE_EXPERTS, logits, -jnp.inf)
    m1 = jnp.max(logits, axis=-1, keepdims=True)
    i1 = jnp.min(jnp.where(logits == m1, lane, LANES), axis=-1, keepdims=True)
    rest = jnp.where(lane == i1, -jnp.inf, logits)
    m2 = jnp.max(rest, axis=-1, keepdims=True)
    i2 = jnp.min(jnp.where(rest == m2, lane, LANES), axis=-1, keepdims=True)
    e2 = jnp.exp(m2 - m1)
    w1 = 1.0 / (1.0 + e2)
    w2 = e2 * w1
    route = jnp.where(lane == 0, i1.astype(F32),
                      jnp.where(lane == 1, i2.astype(F32),
                                jnp.where(lane == 2, w1, jnp.where(lane == 3, w2, 0.0))))
    route_ref[0] = route


def _mla_out_router(h, oT, wo, nf, r, tm):
    B, S, D = h.shape
    tok = lambda w_: pl.BlockSpec((1, tm, w_), lambda b, i: (b, i, 0))
    return pl.pallas_call(
        _mla_out_router_kernel,
        grid=(B, S // tm),
        in_specs=[tok(D), pl.BlockSpec((1, D, tm), lambda b, i: (b, 0, i)),
                  _const_spec(wo.shape), _const_spec(nf.shape), _const_spec(r.shape)],
        out_specs=[tok(D), tok(D), tok(LANES)],
        out_shape=[jax.ShapeDtypeStruct((B, S, D), F32), jax.ShapeDtypeStruct((B, S, D), BF16),
                   jax.ShapeDtypeStruct((B, S, LANES), F32)],
        compiler_params=_params(("parallel", "parallel")),
        name="mla_out_router",
    )(h, oT, wo, nf, r)


def _experts_kernel(te_ref, tv_ref, x_ref, wg_ref, wu_ref, wd_ref, y_ref, acc_sc):
    i = pl.program_id(0)
    f = pl.program_id(1)

    @pl.when(tv_ref[i] > 0)
    def _():
        x = x_ref[...]
        g = _dot(x, wg_ref[0])
        u = _dot(x, wu_ref[0])
        hdn = (g * jax.nn.sigmoid(g) * u).astype(BF16)
        y = _dot(hdn, wd_ref[0])

        @pl.when(f == 0)
        def _():
            acc_sc[...] = y

        @pl.when(f == pl.num_programs(1) - 1)
        def _():
            y_ref[...] = (acc_sc[...] + y).astype(BF16)


def _experts(xs, te, tv, wg, wu, wd, tm, nf):
    P, D = xs.shape
    fc = E_FF // nf
    grid_spec = pltpu.PrefetchScalarGridSpec(
        num_scalar_prefetch=2,
        grid=(P // tm, nf),
        in_specs=[pl.BlockSpec((tm, D), lambda i, f, te, tv: (i, 0)),
                  pl.BlockSpec((1, D, fc), lambda i, f, te, tv: (te[i], 0, f)),
                  pl.BlockSpec((1, D, fc), lambda i, f, te, tv: (te[i], 0, f)),
                  pl.BlockSpec((1, fc, D), lambda i, f, te, tv: (te[i], f, 0))],
        out_specs=pl.BlockSpec((tm, D), lambda i, f, te, tv: (i, 0)),
        scratch_shapes=[pltpu.VMEM((tm, D), F32)],
    )
    return pl.pallas_call(
        _experts_kernel,
        grid_spec=grid_spec,
        out_shape=jax.ShapeDtypeStruct((P, D), BF16),
        compiler_params=_params(("parallel", "arbitrary")),
        name="experts",
    )(te, tv, xs, wg, wu, wd)


def _combine_ple_kernel(h_ref, y0_ref, y1_ref, r_ref, p_ref, gn_ref, pg_ref, pp_ref, out_ref):
    r = r_ref[0]
    h2 = h_ref[0] + r[:, 2:3] * y0_ref[0].astype(F32) + r[:, 3:4] * y1_ref[0].astype(F32)
    out_ref[0] = _ple(h2, p_ref[0], gn_ref[...], pg_ref[...], pp_ref[...])


def _combine_ple(h, y0, y1, route, p, gn, pg, pp, tm):
    B, S, D = h.shape
    tok = lambda w_: pl.BlockSpec((1, tm, w_), lambda b, i: (b, i, 0))
    consts = [gn, pg, pp]
    return pl.pallas_call(
        _combine_ple_kernel,
        grid=(B, S // tm),
        in_specs=[tok(D), tok(D), tok(D), tok(LANES), tok(PLE_DIM)] + [_const_spec(c.shape) for c in consts],
        out_specs=tok(D),
        out_shape=jax.ShapeDtypeStruct((B, S, D), F32),
        compiler_params=_params(("parallel", "parallel")),
        name="combine_ple",
    )(h, y0, y1, route, p, *consts)


def _block_diag_ones(n, blk):
    i = np.arange(n)
    return jnp.asarray((i[:, None] // blk == i[None, :] // blk).astype(np.float32), BF16)


def _prep_even(W, j, i):
    w = W['ab_w_in'][j]
    c = lambda a, b: w[:, a:b]
    k0, k1 = c(512, 576), c(576, 640)
    v0, v1 = c(640, 704), c(704, 768)
    w_in = jnp.concatenate([c(0, 512), k0, k0, k1, k1, v0, v0, v1, v1, c(768, 1280)], axis=1).astype(BF16)
    return dict(
        g=W['norm_mix'][i][None], w_in=w_in, e=_block_diag_ones(256, A_HEAD_DIM),
        qg=(jnp.tile(W['ab_q_norm'][j], A_HEADS) * A_HEAD_DIM ** -0.5)[None],
        kg=jnp.tile(W['ab_k_norm'][j], 2 * A_KV_HEADS)[None],
        sink=W['ab_sink'][j], pool_w=W['ab_pool_w'][j].astype(BF16), pool_scale=W['ab_pool_scale'][j][None],
        wo=W['ab_w_out'][j].astype(BF16), nf=W['norm_ffn'][i][None],
        wg=W['ffn_w_gate'][j].astype(BF16), wu=W['ffn_w_up'][j].astype(BF16), wd=W['ffn_w_down'][j].astype(BF16),
        gn=W['ple_gate_norm'][i][None], pg=W['ple_w_gate'][i].astype(BF16), pp=W['ple_w_proj'][i].astype(BF16))


def _prep_odd(W, j, i):
    pad_h = lambda a, n: jnp.pad(a, [(0, 0)] * (a.ndim - 1) + [(0, n - a.shape[-1])])
    w_in = pad_h(W['mla_w_in'][j], 512).astype(BF16)
    wuq = pad_h(W['mla_w_uq'][j].reshape(C_Q_LORA, C_HEADS, C_QK), LANES).reshape(C_Q_LORA, C_HEADS * LANES)
    gq = jnp.tile(pad_h(W['mla_q_norm'][j], LANES), C_HEADS) * (C_QK ** -0.5 * math.log2(math.e))
    ukv = W['mla_w_ukv'][j].reshape(C_KV_LORA, C_HEADS, C_NOPE + C_V)
    wuk = pad_h(ukv[..., :C_NOPE], LANES).reshape(C_KV_LORA, C_HEADS * LANES)
    wuv = ukv[..., C_NOPE:].reshape(C_KV_LORA, C_HEADS * C_V)
    kn = W['mla_k_norm'][j]
    gkn = jnp.tile(pad_h(kn[:C_NOPE], LANES), C_HEADS)[None]
    gkr = pad_h(kn[C_NOPE:], LANES)[None]
    half = C_ROPE // 2
    rot = np.zeros((LANES, LANES), np.float32)
    rot[np.arange(half) + half, np.arange(half)] = -1.0
    rot[np.arange(half), np.arange(half) + half] = 1.0
    place = np.zeros((LANES, C_HEADS * LANES), np.float32)
    for h in range(C_HEADS):
        place[np.arange(C_ROPE), h * LANES + C_NOPE + np.arange(C_ROPE)] = 1.0
    router = W['moe_router'][j]
    r_hi = router.astype(BF16)
    r_lo = (router - r_hi.astype(F32)).astype(BF16)
    r = pad_h(jnp.concatenate([r_hi, r_lo], axis=1), LANES)
    return dict(
        g=W['norm_mix'][i][None], w_in=w_in, qln=W['mla_q_lat_norm'][j][None], kln=W['mla_kv_lat_norm'][j][None],
        wuq=wuq.T.astype(BF16), gq=gq[:, None], wuk=wuk.astype(BF16), wuv=wuv.T.astype(BF16), gkn=gkn, gkr=gkr,
        rot=jnp.asarray(rot, BF16), place=jnp.asarray(place, BF16), e=_block_diag_ones(2 * LANES, LANES),
        wo=W['mla_w_out'][j].astype(BF16), nf=W['norm_ffn'][i][None], r=r,
        wg=W['moe_w_gate'][j].astype(BF16), wu=W['moe_w_up'][j].astype(BF16), wd=W['moe_w_down'][j].astype(BF16),
        gn=W['ple_gate_norm'][i][None], pg=W['ple_w_gate'][i].astype(BF16), pp=W['ple_w_proj'][i].astype(BF16))


def _rope_tables(S):
    half = C_ROPE // 2
    inv = ROPE_THETA ** (-jnp.arange(half, dtype=F32) / half)
    ang = jnp.arange(S).astype(F32)[:, None] * inv[None, :]
    cos, sin = jnp.cos(ang), jnp.sin(ang)
    wide = lambda t: jnp.pad(jnp.concatenate([t, t], axis=1), ((0, 0), (0, LANES - C_ROPE)))
    return cos.T, sin.T, wide(cos), wide(sin)


def _route_layout(ids, tm):
    n_slots = ids.size
    n_tiles = n_slots // tm + MOE_EXPERTS
    flat = ids.reshape(-1)
    order = jnp.argsort(flat, stable=True).astype(jnp.int32)
    rank = jnp.argsort(order).astype(jnp.int32)
    slot_end = jnp.searchsorted(flat[order], jnp.arange(MOE_EXPERTS, dtype=flat.dtype), side='right')
    slot_end = slot_end.astype(jnp.int32)
    counts = slot_end - jnp.concatenate([jnp.zeros((1,), jnp.int32), slot_end[:-1]])
    slot_start = slot_end - counts
    tiles = (counts + tm - 1) // tm
    tile_end = jnp.cumsum(tiles)
    tile_start = tile_end - tiles
    te = jnp.sum(jnp.arange(n_tiles)[:, None] >= tile_end[None, :], axis=1)
    tv = (te < MOE_EXPERTS).astype(jnp.int32)
    te = jnp.minimum(te, MOE_EXPERTS - 1).astype(jnp.int32)
    p = jnp.arange(n_tiles * tm, dtype=jnp.int32)
    e_p = jnp.repeat(te, tm)
    q = slot_start[e_p] + p - tile_start[e_p] * tm
    valid = (jnp.repeat(tv, tm) > 0) & (q < slot_end[e_p])
    src = jnp.where(valid, order[jnp.clip(q, 0, n_slots - 1)] // 2, 0)
    row_of_slot = tile_start[flat] * tm + rank - slot_start[flat]
    return src, te, tv, row_of_slot.reshape(ids.shape)


def _tiles(S):
    return dict(tm=min(512, S), tq_win=min(512, S), tq=min(ATT_TQ, S), tk=min(ATT_TK, S), te=512, nf=2)


def _trunk(x, p, even, odd):
    B, S, D = x.shape
    t = _tiles(S)
    q, k, v, u = _ab_in(x, even['g'], even['w_in'], even['e'], even['qg'], even['kg'], t['tm'])
    o = _win_pool(q, k, v, u, even['sink'], even['pool_w'], even['pool_scale'], t['tq_win'])
    h = _out_ffn_ple(x, o, p[0], even['wo'], even['nf'], even['wg'], even['wu'], even['wd'],
                     even['gn'], even['pg'], even['pp'], t['tm'])
    cosT, sinT, cos, sin = _rope_tables(S)
    qT, kk, vT = _mla_in(h, odd['g'], odd['w_in'], odd['qln'], odd['kln'], odd['wuq'], odd['gq'], cosT, sinT,
                         odd['wuk'], odd['wuv'], odd['gkn'], odd['gkr'], cos, sin, odd['rot'], odd['place'],
                         odd['e'], t['tm'])
    oT = _mla_attn(qT, kk, vT, t['tq'], t['tk'])
    h1, xn, route = _mla_out_router(h, oT.reshape(B, C_HEADS * C_V, S), odd['wo'], odd['nf'], odd['r'], t['tm'])
    ids = route[..., 0:2].astype(jnp.int32).reshape(B * S, 2)
    src, te, tv, rows = _route_layout(ids, t['te'])
    xs = jnp.take(xn.reshape(B * S, D), src, axis=0)
    ys = _experts(xs, te, tv, odd['wg'], odd['wu'], odd['wd'], t['te'], t['nf'])
    y0 = jnp.take(ys, rows[:, 0], axis=0).reshape(B, S, D)
    y1 = jnp.take(ys, rows[:, 1], axis=0).reshape(B, S, D)
    return _combine_ple(h1, y0, y1, route, p[1], odd['gn'], odd['pg'], odd['pp'], t['tm'])


def kernel(x_prompt, x_sample, p_prompt, p_sample, norm_mix, norm_ffn, ab_w_in, ab_q_norm, ab_k_norm, ab_sink, ab_pool_w, ab_pool_scale, ab_w_out, ffn_w_gate, ffn_w_up, ffn_w_down, mla_w_in, mla_q_lat_norm, mla_kv_lat_norm, mla_w_uq, mla_w_ukv, mla_q_norm, mla_k_norm, mla_w_out, moe_router, moe_w_gate, moe_w_up, moe_w_down, ple_w_proj, ple_gate_norm, ple_w_gate):
    W = dict(norm_mix=norm_mix, norm_ffn=norm_ffn,
             ab_w_in=ab_w_in, ab_q_norm=ab_q_norm, ab_k_norm=ab_k_norm, ab_sink=ab_sink,
             ab_pool_w=ab_pool_w, ab_pool_scale=ab_pool_scale, ab_w_out=ab_w_out,
             ffn_w_gate=ffn_w_gate, ffn_w_up=ffn_w_up, ffn_w_down=ffn_w_down,
             mla_w_in=mla_w_in, mla_q_lat_norm=mla_q_lat_norm, mla_kv_lat_norm=mla_kv_lat_norm,
             mla_w_uq=mla_w_uq, mla_w_ukv=mla_w_ukv, mla_q_norm=mla_q_norm, mla_k_norm=mla_k_norm,
             mla_w_out=mla_w_out, moe_router=moe_router, moe_w_gate=moe_w_gate, moe_w_up=moe_w_up,
             moe_w_down=moe_w_down, ple_w_proj=ple_w_proj, ple_gate_norm=ple_gate_norm,
             ple_w_gate=ple_w_gate)
    even = _prep_even(W, 0, 0)
    odd = _prep_odd(W, 0, 1)
    return (_trunk(x_prompt, p_prompt, even, odd), _trunk(x_sample, p_sample, even, odd))
```

```python
import functools
import math

import numpy as np
import jax
import jax.numpy as jnp
from jax import lax
from jax.experimental import pallas as pl
from jax.experimental.pallas import tpu as pltpu

F32 = jnp.float32
BF16 = jnp.bfloat16

D_MODEL = 1024
EPS = 1e-6
LOG2E = math.log2(math.e)
LANES = 128
VMEM_LIMIT = 56 * 1024 * 1024

A_HEADS, A_KV_HEADS, A_HEAD_DIM, A_WINDOW = 8, 2, 64, 128
A_Q_DIM = A_HEADS * A_HEAD_DIM
B_WIDTH, B_GROUPS, B_GROUP_DIM = 512, 4, 128
B_POOL_SIZES = (2, 4, 8, 16)
POOL_HALO = 16
WIN_BLOCK = 128
C_HEADS, C_NOPE, C_ROPE, C_V = 16, 64, 32, 64
C_QK = C_NOPE + C_ROPE
C_Q_LORA, C_KV_LORA = 256, 128
ROPE_THETA = 10000.0
D_FF, MOE_EXPERTS, E_FF = 2816, 8, 3584
PLE_DIM = 256


def _const_spec(shape):
    nd = len(shape)
    return pl.BlockSpec(shape, lambda *_: (0,) * nd, pipeline_mode=pl.Buffered(1))


def _params(sem):
    return pltpu.CompilerParams(dimension_semantics=sem, vmem_limit_bytes=VMEM_LIMIT)


def _rms(x, g):
    ms = jnp.mean(x * x, axis=-1, keepdims=True)
    return x * lax.rsqrt(ms + EPS) * g


def _dot(a, b):
    return jnp.dot(a, b, preferred_element_type=F32)


def _dot_nt(a, b):
    return lax.dot_general(a, b, (((1,), (1,)), ((), ())), preferred_element_type=F32)


def _dot_tn(a, b):
    return lax.dot_general(a, b, (((0,), (0,)), ((), ())), preferred_element_type=F32)


def _ple(h, p, gnorm, w_gate, w_proj):
    gate = jax.nn.sigmoid(_dot(_rms(h, gnorm).astype(BF16), w_gate))
    return h + gate * _dot(p.astype(BF16), w_proj)


def _ab_in_kernel(h_ref, g_ref, w_ref, e_ref, qg_ref, kg_ref, q_ref, k_ref, v_ref, u_ref):
    xn = _rms(h_ref[0], g_ref[...]).astype(BF16)
    a = _dot(xn, w_ref[...])
    e = e_ref[...]

    def head_norm(t, g):
        ss = _dot((t * t).astype(BF16), e)
        return t * lax.rsqrt(ss * (1.0 / A_HEAD_DIM) + EPS) * g

    q_ref[0, :, 0:256] = head_norm(a[:, 0:256], qg_ref[:, 0:256]).astype(BF16)
    q_ref[0, :, 256:512] = head_norm(a[:, 256:512], qg_ref[:, 256:512]).astype(BF16)
    k_ref[0] = head_norm(a[:, 512:768], kg_ref[...]).astype(BF16)
    v_ref[0] = a[:, 768:1024].astype(BF16)
    u_ref[0] = a[:, 1024:1536].astype(BF16)


def _ab_in(h, g, w, e, qg, kg, tm):
    B, S, D = h.shape
    tok = lambda w_: pl.BlockSpec((1, tm, w_), lambda b, i: (b, i, 0))
    return pl.pallas_call(
        _ab_in_kernel,
        grid=(B, S // tm),
        in_specs=[tok(D), _const_spec(g.shape), _const_spec(w.shape), _const_spec(e.shape),
                  _const_spec(qg.shape), _const_spec(kg.shape)],
        out_specs=[tok(512), tok(256), tok(256), tok(512)],
        out_shape=[jax.ShapeDtypeStruct((B, S, 512), BF16), jax.ShapeDtypeStruct((B, S, 256), BF16),
                   jax.ShapeDtypeStruct((B, S, 256), BF16), jax.ShapeDtypeStruct((B, S, 512), BF16)],
        compiler_params=_params(("parallel", "parallel")),
        name="ab_in",
    )(h, g, w, e, qg, kg)


def _win_pool_kernel(sink_ref, q_ref, kp_ref, km_ref, kn_ref, vp_ref, vm_ref, vn_ref,
                     up_ref, um_ref, un_ref, pw_ref, ps_ref, o_ref, kbuf, vbuf, ubuf, *, seq, tq):
    j = pl.program_id(1)
    nj = pl.num_programs(1)
    W = WIN_BLOCK
    kbuf[0:W] = kp_ref[0]
    kbuf[W:W + tq] = km_ref[0]
    kbuf[W + tq:] = kn_ref[0]
    vbuf[0:W] = vp_ref[0]
    vbuf[W:W + tq] = vm_ref[0]
    vbuf[W + tq:] = vn_ref[0]

    lane = lax.broadcasted_iota(jnp.int32, (1, 2 * LANES), 1) % LANES
    lo = lane < A_HEAD_DIM
    kb = kbuf[...]
    vb = vbuf[...]
    zero = jnp.zeros_like(kb)
    k_par = (jnp.where(lo, kb, zero), jnp.where(lo, zero, kb))
    v_par = (jnp.where(lo, vb, zero), jnp.where(lo, zero, vb))

    qi = lax.broadcasted_iota(jnp.int32, (W, 3 * W), 0)
    kj = lax.broadcasted_iota(jnp.int32, (W, 3 * W), 1)
    dist = jnp.abs(qi - kj + W)
    distf = dist.astype(F32)
    in_band = dist <= A_WINDOW
    G = A_HEADS // A_KV_HEADS
    bias = [jnp.where(in_band, (-LOG2E * 2.0 ** (-8.0 * (h + 1) / A_HEADS)) * distf, -jnp.inf)
            for h in range(A_HEADS)]
    n_sb = tq // W
    for sb in range(n_sb):
        rows = slice(sb * W, (sb + 3) * W)
        edge = None
        if sb == 0 or sb == n_sb - 1:
            key_pos = j * tq + (sb - 1) * W + kj
            edge = jnp.where((key_pos >= 0) & (key_pos < seq), 0.0, -jnp.inf)
        scores = []
        for h in range(A_HEADS):
            pair, par = divmod(h, 2)
            cols = slice((h // G) * LANES, (h // G + 1) * LANES)
            qp = q_ref[0, sb * W:(sb + 1) * W, pair * LANES:(pair + 1) * LANES]
            scores.append(_dot_nt(qp, k_par[par][rows, cols]))
        probs, inv_den = [], []
        for h in range(A_HEADS):
            sink = sink_ref[h] * LOG2E
            s = scores[h] + bias[h]
            if edge is not None:
                s = s + edge
            m = jnp.maximum(jnp.max(s, axis=-1, keepdims=True), sink)
            ex = jnp.exp2(s - m)
            den = jnp.sum(ex, axis=-1, keepdims=True) + jnp.exp2(sink - m)
            probs.append(ex.astype(BF16))
            inv_den.append(1.0 / den)
        for pair in range(A_HEADS // 2):
            cols = slice(((2 * pair) // G) * LANES, ((2 * pair) // G + 1) * LANES)
            o = (_dot(probs[2 * pair], v_par[0][rows, cols]) * inv_den[2 * pair]
                 + _dot(probs[2 * pair + 1], v_par[1][rows, cols]) * inv_den[2 * pair + 1])
            o_ref[0, sb * W:(sb + 1) * W, pair * LANES:(pair + 1) * LANES] = o.astype(BF16)

    H = POOL_HALO
    ubuf[0:H] = jnp.where(j > 0, up_ref[0].astype(F32), 0.0)
    ubuf[H:H + tq] = um_ref[0].astype(F32)
    ubuf[H + tq:] = jnp.where(j < nj - 1, un_ref[0].astype(F32), 0.0)
    t = j * tq + lax.broadcasted_iota(jnp.int32, (tq, 1), 0)
    for g, w in enumerate(B_POOL_SIZES):
        half = w // 2
        cols = slice(g * B_GROUP_DIM, (g + 1) * B_GROUP_DIM)
        tot = jnp.zeros((tq, B_GROUP_DIM), F32)
        for d in range(-half, half):
            tot = tot + ubuf[H + d:H + d + tq, cols]
        cnt = (jnp.minimum(t + half, seq) - jnp.maximum(t - half, 0)).astype(F32)
        dlt = (tot / cnt - ubuf[H:H + tq, cols]).astype(BF16)
        y = _dot(dlt, pw_ref[g]) * ps_ref[:, cols]
        o_ref[0, :, A_Q_DIM + g * B_GROUP_DIM:A_Q_DIM + (g + 1) * B_GROUP_DIM] = y.astype(BF16)


def _win_pool(q, k, v, u, sink, pool_w, pool_scale, tq):
    B, S, _ = q.shape
    W, H = WIN_BLOCK, POOL_HALO
    nw, nh = S // W, S // H
    rw, rh = tq // W, tq // H
    main = lambda w_: pl.BlockSpec((1, tq, w_), lambda b, j: (b, j, 0))
    prev = lambda rows, r, w_: pl.BlockSpec((1, rows, w_), lambda b, j: (b, jnp.maximum(j * r - 1, 0), 0))
    nxt = lambda rows, r, n, w_: pl.BlockSpec((1, rows, w_), lambda b, j: (b, jnp.minimum((j + 1) * r, n - 1), 0))
    kernel = functools.partial(_win_pool_kernel, seq=S, tq=tq)
    return pl.pallas_call(
        kernel,
        grid=(B, S // tq),
        in_specs=[pl.BlockSpec(memory_space=pltpu.SMEM), main(512),
                  prev(W, rw, 256), main(256), nxt(W, rw, nw, 256),
                  prev(W, rw, 256), main(256), nxt(W, rw, nw, 256),
                  prev(H, rh, 512), main(512), nxt(H, rh, nh, 512),
                  _const_spec(pool_w.shape), _const_spec(pool_scale.shape)],
        out_specs=main(1024),
        out_shape=jax.ShapeDtypeStruct((B, S, 1024), BF16),
        scratch_shapes=[pltpu.VMEM((tq + 2 * W, 256), BF16), pltpu.VMEM((tq + 2 * W, 256), BF16),
                        pltpu.VMEM((tq + 2 * H, 512), F32)],
        compiler_params=_params(("parallel", "parallel")),
        name="win_pool",
    )(sink, q, k, k, k, v, v, v, u, u, u, pool_w, pool_scale)


def _out_ffn_ple_kernel(h_ref, o_ref, p_ref, wo_ref, nf_ref, wg_ref, wu_ref, wd_ref,
                        gn_ref, pg_ref, pp_ref, out_ref, *, ff_chunk):
    h1 = h_ref[0] + _dot(o_ref[0], wo_ref[...])
    xn = _rms(h1, nf_ref[...]).astype(BF16)
    acc = jnp.zeros_like(h1)
    for c0 in range(0, D_FF, ff_chunk):
        g = _dot(xn, wg_ref[:, c0:c0 + ff_chunk])
        u = _dot(xn, wu_ref[:, c0:c0 + ff_chunk])
        hdn = (g * jax.nn.sigmoid(g) * u).astype(BF16)
        acc = acc + _dot(hdn, wd_ref[c0:c0 + ff_chunk, :])
    out_ref[0] = _ple(h1 + acc, p_ref[0], gn_ref[...], pg_ref[...], pp_ref[...])


def _out_ffn_ple(h, o, p, wo, nf, wg, wu, wd, gn, pg, pp, tm):
    B, S, D = h.shape
    tok = lambda w_: pl.BlockSpec((1, tm, w_), lambda b, i: (b, i, 0))
    consts = [wo, nf, wg, wu, wd, gn, pg, pp]
    return pl.pallas_call(
        functools.partial(_out_ffn_ple_kernel, ff_chunk=D_FF // 2),
        grid=(B, S // tm),
        in_specs=[tok(D), tok(D), tok(PLE_DIM)] + [_const_spec(c.shape) for c in consts],
        out_specs=tok(D),
        out_shape=jax.ShapeDtypeStruct((B, S, D), F32),
        compiler_params=_params(("parallel", "parallel")),
        name="out_ffn_ple",
    )(h, o, p, *consts)


def _mla_in_kernel(h_ref, g_ref, w_ref, qln_ref, kln_ref, wuq_ref, gq_ref, cosT_ref, sinT_ref,
                   wuk_ref, wuv_ref, gkn_ref, gkr_ref, cos_ref, sin_ref, rot_ref, place_ref, e_ref,
                   qT_ref, k_ref, vT_ref):
    xn = _rms(h_ref[0], g_ref[...]).astype(BF16)
    a = _dot(xn, w_ref[...])
    cq = _rms(a[:, 0:C_Q_LORA], qln_ref[...]).astype(BF16)
    ckv = _rms(a[:, C_Q_LORA:C_Q_LORA + C_KV_LORA], kln_ref[...]).astype(BF16)
    kpe = a[:, C_Q_LORA + C_KV_LORA:]

    qT = _dot_nt(wuq_ref[...], cq)
    cosT = cosT_ref[...]
    sinT = sinT_ref[...]
    half = C_ROPE // 2
    for h in range(C_HEADS):
        blk = qT[h * LANES:(h + 1) * LANES]
        ss = jnp.sum(blk * blk, axis=0, keepdims=True)
        qn = blk * lax.rsqrt(ss * (1.0 / C_QK) + EPS) * gq_ref[h * LANES:(h + 1) * LANES]
        x1 = qn[C_NOPE:C_NOPE + half]
        x2 = qn[C_NOPE + half:C_QK]
        out = jnp.concatenate([qn[0:C_NOPE], x1 * cosT - x2 * sinT, x1 * sinT + x2 * cosT, qn[C_QK:]], axis=0)
        qT_ref[0, h] = out.astype(BF16)

    vT = _dot_nt(wuv_ref[...], ckv)
    ones = jnp.ones((ATT_ONES, vT.shape[1]), BF16)
    for h in range(C_HEADS):
        vT_ref[0, h, 0:C_V] = vT[h * C_V:(h + 1) * C_V].astype(BF16)
        vT_ref[0, h, C_V:] = ones

    kraw = _dot(ckv, wuk_ref[...])
    kg = kpe * gkr_ref[...]
    krope = kg * cos_ref[...] + _dot(kg.astype(BF16), rot_ref[...]) * sin_ref[...]
    placed = _dot(krope.astype(BF16), place_ref[...])
    ss_pe = jnp.sum(kpe * kpe, axis=-1, keepdims=True)
    e = e_ref[...]
    for h2 in range(C_HEADS // 2):
        cols = slice(h2 * 2 * LANES, (h2 + 1) * 2 * LANES)
        kr = kraw[:, cols]
        ss = _dot((kr * kr).astype(BF16), e) + ss_pe
        kn = (kr * gkn_ref[:, cols] + placed[:, cols]) * lax.rsqrt(ss * (1.0 / C_QK) + EPS)
        k_ref[0, 2 * h2] = kn[:, 0:LANES].astype(BF16)
        k_ref[0, 2 * h2 + 1] = kn[:, LANES:].astype(BF16)


def _mla_in(h, g, w, qln, kln, wuq, gq, cosT, sinT, wuk, wuv, gkn, gkr, cos, sin, rot, place, e, tm):
    B, S, D = h.shape
    consts_a = [g, w, qln, kln, wuq, gq]
    consts_b = [wuk, wuv, gkn, gkr]
    consts_c = [rot, place, e]
    return pl.pallas_call(
        _mla_in_kernel,
        grid=(B, S // tm),
        in_specs=[pl.BlockSpec((1, tm, D), lambda b, i: (b, i, 0))]
        + [_const_spec(c.shape) for c in consts_a]
        + [pl.BlockSpec((C_ROPE // 2, tm), lambda b, i: (0, i))] * 2
        + [_const_spec(c.shape) for c in consts_b]
        + [pl.BlockSpec((tm, LANES), lambda b, i: (i, 0))] * 2
        + [_const_spec(c.shape) for c in consts_c],
        out_specs=[pl.BlockSpec((1, C_HEADS, LANES, tm), lambda b, i: (b, 0, 0, i)),
                   pl.BlockSpec((1, C_HEADS, tm, LANES), lambda b, i: (b, 0, i, 0)),
                   pl.BlockSpec((1, C_HEADS, C_V + ATT_ONES, tm), lambda b, i: (b, 0, 0, i))],
        out_shape=[jax.ShapeDtypeStruct((B, C_HEADS, LANES, S), BF16),
                   jax.ShapeDtypeStruct((B, C_HEADS, S, LANES), BF16),
                   jax.ShapeDtypeStruct((B, C_HEADS, C_V + ATT_ONES, S), BF16)],
        compiler_params=_params(("parallel", "parallel")),
        name="mla_in",
    )(h, *consts_a, cosT, sinT, *consts_b, cos, sin, *consts_c)


ATT_STRIP = 256
ATT_SUB = 128
ATT_ONES = 16
ATT_AHEAD = 12
ATT_TQ = 1024
ATT_TK = 2048


def _mla_attn_kernel(qT_ref, k_ref, vT_ref, oT_ref, m_sc, acc_sc, *, tk):
    seq = k_ref.shape[2]
    tq = qT_ref.shape[3]
    m_sc[...] = jnp.full_like(m_sc, -jnp.inf)
    acc_sc[...] = jnp.zeros_like(acc_sc)

    items = [(st, j) for j in range(tk // ATT_SUB) for st in range(tq // ATT_STRIP)]

    def body(c, carry):
        def key_off(j):
            return pl.multiple_of(c * tk + j * ATT_SUB, ATT_SUB)

        def scores(item):
            st, j = item
            return _dot(k_ref[0, 0, pl.ds(key_off(j), ATT_SUB), :],
                        qT_ref[0, 0, :, st * ATT_STRIP:(st + 1) * ATT_STRIP])

        pending = [scores(it) for it in items[:ATT_AHEAD]]
        for idx, (st, j) in enumerate(items):
            if idx + ATT_AHEAD < len(items):
                pending.append(scores(items[idx + ATT_AHEAD]))
            s = pending.pop(0)
            cols = slice(st * ATT_STRIP, (st + 1) * ATT_STRIP)
            m = m_sc[:, cols]
            m_new = jnp.maximum(m, jnp.max(s, axis=0, keepdims=True))
            p = jnp.exp2(s - m_new).astype(BF16)
            v1 = vT_ref[0, 0, :, pl.ds(key_off(j), ATT_SUB)]
            acc_sc[:, cols] = jnp.exp2(m - m_new) * acc_sc[:, cols] + _dot(v1, p)
            m_sc[:, cols] = m_new
        return carry

    lax.fori_loop(0, seq // tk, body, 0)
    acc = acc_sc[...]
    oT_ref[0, 0] = (acc[0:C_V] / acc[C_V:C_V + 1]).astype(BF16)


def _mla_attn(qT, k, vT, tq, tk):
    B, H, _, S = qT.shape
    return pl.pallas_call(
        functools.partial(_mla_attn_kernel, tk=tk),
        grid=(B, H, S // tq),
        in_specs=[pl.BlockSpec((1, 1, LANES, tq), lambda b, h, i: (b, h, 0, i)),
                  pl.BlockSpec((1, 1, S, LANES), lambda b, h, i: (b, h, 0, 0)),
                  pl.BlockSpec((1, 1, C_V + ATT_ONES, S), lambda b, h, i: (b, h, 0, 0))],
        out_specs=pl.BlockSpec((1, 1, C_V, tq), lambda b, h, i: (b, h, 0, i)),
        out_shape=jax.ShapeDtypeStruct((B, H, C_V, S), BF16),
        scratch_shapes=[pltpu.VMEM((1, tq), F32), pltpu.VMEM((C_V + ATT_ONES, tq), F32)],
        compiler_params=_params(("parallel", "parallel", "arbitrary")),
        name="mla_attn",
    )(qT, k, vT)


def _mla_out_router_kernel(h_ref, oT_ref, wo_ref, nf_ref, r_ref, h1_ref, xn_ref, route_ref):
    h1 = h_ref[0] + _dot_tn(oT_ref[0], wo_ref[...])
    h1_ref[0] = h1
    xn = _rms(h1, nf_ref[...])
    hi = xn.astype(BF16)
    xn_ref[0] = hi
    lo = (xn - hi.astype(F32)).astype(BF16)
    p_hi = _dot(hi, r_ref[...])
    p_lo = _dot(lo, r_ref[...])
    logits = p_hi + p_lo + pltpu.roll(p_hi, LANES - MOE_EXPERTS, axis=1)
    lane = lax.broadcasted_iota(jnp.int32, logits.shape, 1)
    logits = jnp.where(lane < MOE_EXPERTS, logits, -jnp.inf)
    m1 = jnp.max(logits, axis=-1, keepdims=True)
    i1 = jnp.min(jnp.where(logits == m1, lane, LANES), axis=-1, keepdims=True)
    rest = jnp.where(lane == i1, -jnp.inf, logits)
    m2 = jnp.max(rest, axis=-1, keepdims=True)
    i2 = jnp.min(jnp.where(rest == m2, lane, LANES), axis=-1, keepdims=True)
    e2 = jnp.exp(m2 - m1)
    w1 = 1.0 / (1.0 + e2)
    w2 = e2 * w1
    route = jnp.where(lane == 0, i1.astype(F32),
                      jnp.where(lane == 1, i2.astype(F32),
                                jnp.where(lane == 2, w1, jnp.where(lane == 3, w2, 0.0))))
    route_ref[0] = route


def _mla_out_router(h, oT, wo, nf, r, tm):
    B, S, D = h.shape
    tok = lambda w_: pl.BlockSpec((1, tm, w_), lambda b, i: (b, i, 0))
    return pl.pallas_call(
        _mla_out_router_kernel,
        grid=(B, S // tm),
        in_specs=[tok(D), pl.BlockSpec((1, D, tm), lambda b, i: (b, 0, i)),
                  _const_spec(wo.shape), _const_spec(nf.shape), _const_spec(r.shape)],
        out_specs=[tok(D), tok(D), tok(LANES)],
        out_shape=[jax.ShapeDtypeStruct((B, S, D), F32), jax.ShapeDtypeStruct((B, S, D), BF16),
                   jax.ShapeDtypeStruct((B, S, LANES), F32)],
        compiler_params=_params(("parallel", "parallel")),
        name="mla_out_router",
    )(h, oT, wo, nf, r)


def _experts_kernel(te_ref, tv_ref, x_ref, wg_ref, wu_ref, wd_ref, y_ref, acc_sc):
    i = pl.program_id(0)
    f = pl.program_id(1)

    @pl.when(tv_ref[i] > 0)
    def _():
        x = x_ref[...]
        g = _dot(x, wg_ref[0])
        u = _dot(x, wu_ref[0])
        hdn = (g * jax.nn.sigmoid(g) * u).astype(BF16)
        y = _dot(hdn, wd_ref[0])

        @pl.when(f == 0)
        def _():
            acc_sc[...] = y

        @pl.when(f == pl.num_programs(1) - 1)
        def _():
            y_ref[...] = (acc_sc[...] + y).astype(BF16)


def _experts(xs, te, tv, wg, wu, wd, tm, nf):
    P, D = xs.shape
    fc = E_FF // nf
    grid_spec = pltpu.PrefetchScalarGridSpec(
        num_scalar_prefetch=2,
        grid=(P // tm, nf),
        in_specs=[pl.BlockSpec((tm, D), lambda i, f, te, tv: (i, 0)),
                  pl.BlockSpec((1, D, fc), lambda i, f, te, tv: (te[i], 0, f)),
                  pl.BlockSpec((1, D, fc), lambda i, f, te, tv: (te[i], 0, f)),
                  pl.BlockSpec((1, fc, D), lambda i, f, te, tv: (te[i], f, 0))],
        out_specs=pl.BlockSpec((tm, D), lambda i, f, te, tv: (i, 0)),
        scratch_shapes=[pltpu.VMEM((tm, D), F32)],
    )
    return pl.pallas_call(
        _experts_kernel,
        grid_spec=grid_spec,
        out_shape=jax.ShapeDtypeStruct((P, D), BF16),
        compiler_params=_params(("parallel", "arbitrary")),
        name="experts",
    )(te, tv, xs, wg, wu, wd)


def _combine_ple_kernel(h_ref, y0_ref, y1_ref, r_ref, p_ref, gn_ref, pg_ref, pp_ref, out_ref):
    r = r_ref[0]
    h2 = h_ref[0] + r[:, 2:3] * y0_ref[0].astype(F32) + r[:, 3:4] * y1_ref[0].astype(F32)
    out_ref[0] = _ple(h2, p_ref[0], gn_ref[...], pg_ref[...], pp_ref[...])


def _combine_ple(h, y0, y1, route, p, gn, pg, pp, tm):
    B, S, D = h.shape
    tok = lambda w_: pl.BlockSpec((1, tm, w_), lambda b, i: (b, i, 0))
    consts = [gn, pg, pp]
    return pl.pallas_call(
        _combine_ple_kernel,
        grid=(B, S // tm),
        in_specs=[tok(D), tok(D), tok(D), tok(LANES), tok(PLE_DIM)] + [_const_spec(c.shape) for c in consts],
        out_specs=tok(D),
        out_shape=jax.ShapeDtypeStruct((B, S, D), F32),
        compiler_params=_params(("parallel", "parallel")),
        name="combine_ple",
    )(h, y0, y1, route, p, *consts)


def _block_diag_ones(n, blk):
    i = np.arange(n)
    return jnp.asarray((i[:, None] // blk == i[None, :] // blk).astype(np.float32), BF16)


def _prep_even(W, j, i):
    w = W['ab_w_in'][j]
    c = lambda a, b: w[:, a:b]
    k0, k1 = c(512, 576), c(576, 640)
    v0, v1 = c(640, 704), c(704, 768)
    w_in = jnp.concatenate([c(0, 512), k0, k0, k1, k1, v0, v0, v1, v1, c(768, 1280)], axis=1).astype(BF16)
    return dict(
        g=W['norm_mix'][i][None], w_in=w_in, e=_block_diag_ones(256, A_HEAD_DIM),
        qg=(jnp.tile(W['ab_q_norm'][j], A_HEADS) * (A_HEAD_DIM ** -0.5 * LOG2E))[None],
        kg=jnp.tile(W['ab_k_norm'][j], 2 * A_KV_HEADS)[None],
        sink=W['ab_sink'][j], pool_w=W['ab_pool_w'][j].astype(BF16), pool_scale=W['ab_pool_scale'][j][None],
        wo=W['ab_w_out'][j].astype(BF16), nf=W['norm_ffn'][i][None],
        wg=W['ffn_w_gate'][j].astype(BF16), wu=W['ffn_w_up'][j].astype(BF16), wd=W['ffn_w_down'][j].astype(BF16),
        gn=W['ple_gate_norm'][i][None], pg=W['ple_w_gate'][i].astype(BF16), pp=W['ple_w_proj'][i].astype(BF16))


def _prep_odd(W, j, i):
    pad_h = lambda a, n: jnp.pad(a, [(0, 0)] * (a.ndim - 1) + [(0, n - a.shape[-1])])
    w_in = pad_h(W['mla_w_in'][j], 512).astype(BF16)
    wuq = pad_h(W['mla_w_uq'][j].reshape(C_Q_LORA, C_HEADS, C_QK), LANES).reshape(C_Q_LORA, C_HEADS * LANES)
    gq = jnp.tile(pad_h(W['mla_q_norm'][j], LANES), C_HEADS) * (C_QK ** -0.5 * math.log2(math.e))
    ukv = W['mla_w_ukv'][j].reshape(C_KV_LORA, C_HEADS, C_NOPE + C_V)
    wuk = pad_h(ukv[..., :C_NOPE], LANES).reshape(C_KV_LORA, C_HEADS * LANES)
    wuv = ukv[..., C_NOPE:].reshape(C_KV_LORA, C_HEADS * C_V)
    kn = W['mla_k_norm'][j]
    gkn = jnp.tile(pad_h(kn[:C_NOPE], LANES), C_HEADS)[None]
    gkr = pad_h(kn[C_NOPE:], LANES)[None]
    half = C_ROPE // 2
    rot = np.zeros((LANES, LANES), np.float32)
    rot[np.arange(half) + half, np.arange(half)] = -1.0
    rot[np.arange(half), np.arange(half) + half] = 1.0
    place = np.zeros((LANES, C_HEADS * LANES), np.float32)
    for h in range(C_HEADS):
        place[np.arange(C_ROPE), h * LANES + C_NOPE + np.arange(C_ROPE)] = 1.0
    router = W['moe_router'][j]
    r_hi = router.astype(BF16)
    r_lo = (router - r_hi.astype(F32)).astype(BF16)
    r = pad_h(jnp.concatenate([r_hi, r_lo], axis=1), LANES)
    return dict(
        g=W['norm_mix'][i][None], w_in=w_in, qln=W['mla_q_lat_norm'][j][None], kln=W['mla_kv_lat_norm'][j][None],
        wuq=wuq.T.astype(BF16), gq=gq[:, None], wuk=wuk.astype(BF16), wuv=wuv.T.astype(BF16), gkn=gkn, gkr=gkr,
        rot=jnp.asarray(rot, BF16), place=jnp.asarray(place, BF16), e=_block_diag_ones(2 * LANES, LANES),
        wo=W['mla_w_out'][j].astype(BF16), nf=W['norm_ffn'][i][None], r=r,
        wg=W['moe_w_gate'][j].astype(BF16), wu=W['moe_w_up'][j].astype(BF16), wd=W['moe_w_down'][j].astype(BF16),
        gn=W['ple_gate_norm'][i][None], pg=W['ple_w_gate'][i].astype(BF16), pp=W['ple_w_proj'][i].astype(BF16))


def _rope_tables(S):
    half = C_ROPE // 2
    inv = ROPE_THETA ** (-jnp.arange(half, dtype=F32) / half)
    ang = jnp.arange(S).astype(F32)[:, None] * inv[None, :]
    cos, sin = jnp.cos(ang), jnp.sin(ang)
    wide = lambda t: jnp.pad(jnp.concatenate([t, t], axis=1), ((0, 0), (0, LANES - C_ROPE)))
    return cos.T, sin.T, wide(cos), wide(sin)


def _route_layout(ids, tm):
    n_slots = ids.size
    n_tiles = n_slots // tm + MOE_EXPERTS
    flat = ids.reshape(-1)
    order = jnp.argsort(flat, stable=True).astype(jnp.int32)
    rank = jnp.argsort(order).astype(jnp.int32)
    slot_end = jnp.searchsorted(flat[order], jnp.arange(MOE_EXPERTS, dtype=flat.dtype), side='right')
    slot_end = slot_end.astype(jnp.int32)
    counts = slot_end - jnp.concatenate([jnp.zeros((1,), jnp.int32), slot_end[:-1]])
    slot_start = slot_end - counts
    tiles = (counts + tm - 1) // tm
    tile_end = jnp.cumsum(tiles)
    tile_start = tile_end - tiles
    te = jnp.sum(jnp.arange(n_tiles)[:, None] >= tile_end[None, :], axis=1)
    tv = (te < MOE_EXPERTS).astype(jnp.int32)
    te = jnp.minimum(te, MOE_EXPERTS - 1).astype(jnp.int32)
    p = jnp.arange(n_tiles * tm, dtype=jnp.int32)
    e_p = jnp.repeat(te, tm)
    q = slot_start[e_p] + p - tile_start[e_p] * tm
    valid = (jnp.repeat(tv, tm) > 0) & (q < slot_end[e_p])
    src = jnp.where(valid, order[jnp.clip(q, 0, n_slots - 1)] // 2, 0)
    row_of_slot = tile_start[flat] * tm + rank - slot_start[flat]
    return src, te, tv, row_of_slot.reshape(ids.shape)


def _tiles(S):
    return dict(tm=min(512, S), tq_win=min(512, S), tq=min(ATT_TQ, S), tk=min(ATT_TK, S), te=512, nf=2)


def _trunk(x, p, even, odd):
    B, S, D = x.shape
    t = _tiles(S)
    q, k, v, u = _ab_in(x, even['g'], even['w_in'], even['e'], even['qg'], even['kg'], t['tm'])
    o = _win_pool(q, k, v, u, even['sink'], even['pool_w'], even['pool_scale'], t['tq_win'])
    h = _out_ffn_ple(x, o, p[0], even['wo'], even['nf'], even['wg'], even['wu'], even['wd'],
                     even['gn'], even['pg'], even['pp'], t['tm'])
    cosT, sinT, cos, sin = _rope_tables(S)
    qT, kk, vT = _mla_in(h, odd['g'], odd['w_in'], odd['qln'], odd['kln'], odd['wuq'], odd['gq'], cosT, sinT,
                         odd['wuk'], odd['wuv'], odd['gkn'], odd['gkr'], cos, sin, odd['rot'], odd['place'],
                         odd['e'], t['tm'])
    oT = _mla_attn(qT, kk, vT, t['tq'], t['tk'])
    h1, xn, route = _mla_out_router(h, oT.reshape(B, C_HEADS * C_V, S), odd['wo'], odd['nf'], odd['r'], t['tm'])
    ids = route[..., 0:2].astype(jnp.int32).reshape(B * S, 2)
    src, te, tv, rows = _route_layout(ids, t['te'])
    xs = jnp.take(xn.reshape(B * S, D), src, axis=0)
    ys = _experts(xs, te, tv, odd['wg'], odd['wu'], odd['wd'], t['te'], t['nf'])
    y0 = jnp.take(ys, rows[:, 0], axis=0).reshape(B, S, D)
    y1 = jnp.take(ys, rows[:, 1], axis=0).reshape(B, S, D)
    return _combine_ple(h1, y0, y1, route, p[1], odd['gn'], odd['pg'], odd['pp'], t['tm'])


def kernel(x_prompt, x_sample, p_prompt, p_sample, norm_mix, norm_ffn, ab_w_in, ab_q_norm, ab_k_norm, ab_sink, ab_pool_w, ab_pool_scale, ab_w_out, ffn_w_gate, ffn_w_up, ffn_w_down, mla_w_in, mla_q_lat_norm, mla_kv_lat_norm, mla_w_uq, mla_w_ukv, mla_q_norm, mla_k_norm, mla_w_out, moe_router, moe_w_gate, moe_w_up, moe_w_down, ple_w_proj, ple_gate_norm, ple_w_gate):
    W = dict(norm_mix=norm_mix, norm_ffn=norm_ffn,
             ab_w_in=ab_w_in, ab_q_norm=ab_q_norm, ab_k_norm=ab_k_norm, ab_sink=ab_sink,
             ab_pool_w=ab_pool_w, ab_pool_scale=ab_pool_scale, ab_w_out=ab_w_out,
             ffn_w_gate=ffn_w_gate, ffn_w_up=ffn_w_up, ffn_w_down=ffn_w_down,
             mla_w_in=mla_w_in, mla_q_lat_norm=mla_q_lat_norm, mla_kv_lat_norm=mla_kv_lat_norm,
             mla_w_uq=mla_w_uq, mla_w_ukv=mla_w_ukv, mla_q_norm=mla_q_norm, mla_k_norm=mla_k_norm,
             mla_w_out=mla_w_out, moe_router=moe_router, moe_w_gate=moe_w_gate, moe_w_up=moe_w_up,
             moe_w_down=moe_w_down, ple_w_proj=ple_w_proj, ple_gate_norm=ple_gate_norm,
             ple_w_gate=ple_w_gate)
    even = _prep_even(W, 0, 0)
    odd = _prep_odd(W, 0, 1)
    return (_trunk(x_prompt, p_prompt, even, odd), _trunk(x_sample, p_sample, even, odd))
```

```python
import functools
import math

import numpy as np
import jax
import jax.numpy as jnp
from jax import lax
from jax.experimental import pallas as pl
from jax.experimental.pallas import tpu as pltpu

F32 = jnp.float32
BF16 = jnp.bfloat16

D_MODEL = 1024
EPS = 1e-6
LOG2E = math.log2(math.e)
LANES = 128
VMEM_LIMIT = 56 * 1024 * 1024

A_HEADS, A_KV_HEADS, A_HEAD_DIM, A_WINDOW = 8, 2, 64, 128
A_Q_DIM = A_HEADS * A_HEAD_DIM
B_WIDTH, B_GROUPS, B_GROUP_DIM = 512, 4, 128
B_POOL_SIZES = (2, 4, 8, 16)
POOL_HALO = 16
WIN_BLOCK = 128
C_HEADS, C_NOPE, C_ROPE, C_V = 16, 64, 32, 64
C_QK = C_NOPE + C_ROPE
C_Q_LORA, C_KV_LORA = 256, 128
ROPE_THETA = 10000.0
D_FF, MOE_EXPERTS, E_FF = 2816, 8, 3584
PLE_DIM = 256


def _const_spec(shape):
    nd = len(shape)
    return pl.BlockSpec(shape, lambda *_: (0,) * nd, pipeline_mode=pl.Buffered(1))


def _params(sem):
    return pltpu.CompilerParams(dimension_semantics=sem, vmem_limit_bytes=VMEM_LIMIT)


def _rms(x, g):
    ms = jnp.mean(x * x, axis=-1, keepdims=True)
    return x * lax.rsqrt(ms + EPS) * g


def _dot(a, b):
    return jnp.dot(a, b, preferred_element_type=F32)


def _dot_nt(a, b):
    return lax.dot_general(a, b, (((1,), (1,)), ((), ())), preferred_element_type=F32)


def _dot_tn(a, b):
    return lax.dot_general(a, b, (((0,), (0,)), ((), ())), preferred_element_type=F32)


def _ple(h, p, gnorm, w_gate, w_proj):
    gate = jax.nn.sigmoid(_dot(_rms(h, gnorm).astype(BF16), w_gate))
    return h + gate * _dot(p.astype(BF16), w_proj)


def _ab_in_kernel(h_ref, g_ref, w_ref, e_ref, qg_ref, kg_ref, q_ref, k_ref, v_ref, u_ref):
    xn = _rms(h_ref[0], g_ref[...]).astype(BF16)
    a = _dot(xn, w_ref[...])
    e = e_ref[...]

    def head_norm(t, g):
        ss = _dot((t * t).astype(BF16), e)
        return t * lax.rsqrt(ss * (1.0 / A_HEAD_DIM) + EPS) * g

    q_ref[0, :, 0:256] = head_norm(a[:, 0:256], qg_ref[:, 0:256]).astype(BF16)
    q_ref[0, :, 256:512] = head_norm(a[:, 256:512], qg_ref[:, 256:512]).astype(BF16)
    k_ref[0] = head_norm(a[:, 512:768], kg_ref[...]).astype(BF16)
    v_ref[0] = a[:, 768:1024].astype(BF16)
    u_ref[0] = a[:, 1024:1536].astype(BF16)


def _ab_in(h, g, w, e, qg, kg, tm):
    B, S, D = h.shape
    tok = lambda w_: pl.BlockSpec((1, tm, w_), lambda b, i: (b, i, 0))
    return pl.pallas_call(
        _ab_in_kernel,
        grid=(B, S // tm),
        in_specs=[tok(D), _const_spec(g.shape), _const_spec(w.shape), _const_spec(e.shape),
                  _const_spec(qg.shape), _const_spec(kg.shape)],
        out_specs=[tok(512), tok(256), tok(256), tok(512)],
        out_shape=[jax.ShapeDtypeStruct((B, S, 512), BF16), jax.ShapeDtypeStruct((B, S, 256), BF16),
                   jax.ShapeDtypeStruct((B, S, 256), BF16), jax.ShapeDtypeStruct((B, S, 512), BF16)],
        compiler_params=_params(("parallel", "parallel")),
        name="ab_in",
    )(h, g, w, e, qg, kg)


def _win_pool_kernel(sink_ref, q_ref, kp_ref, km_ref, kn_ref, vp_ref, vm_ref, vn_ref,
                     up_ref, um_ref, un_ref, pw_ref, ps_ref, o_ref, kbuf, vbuf, ubuf, *, seq, tq):
    j = pl.program_id(1)
    nj = pl.num_programs(1)
    W = WIN_BLOCK
    kbuf[0:W] = kp_ref[0]
    kbuf[W:W + tq] = km_ref[0]
    kbuf[W + tq:] = kn_ref[0]
    vbuf[0:W] = vp_ref[0]
    vbuf[W:W + tq] = vm_ref[0]
    vbuf[W + tq:] = vn_ref[0]

    lane = lax.broadcasted_iota(jnp.int32, (1, 2 * LANES), 1) % LANES
    lo = lane < A_HEAD_DIM
    kb = kbuf[...]
    vb = vbuf[...]
    zero = jnp.zeros_like(kb)
    k_par = (jnp.where(lo, kb, zero), jnp.where(lo, zero, kb))
    v_par = (jnp.where(lo, vb, zero), jnp.where(lo, zero, vb))

    qi = lax.broadcasted_iota(jnp.int32, (W, 3 * W), 0)
    kj = lax.broadcasted_iota(jnp.int32, (W, 3 * W), 1)
    dist = jnp.abs(qi - kj + W)
    distf = dist.astype(F32)
    in_band = dist <= A_WINDOW
    G = A_HEADS // A_KV_HEADS
    bias = [jnp.where(in_band, (-LOG2E * 2.0 ** (-8.0 * (h + 1) / A_HEADS)) * distf, -jnp.inf)
            for h in range(A_HEADS)]
    n_sb = tq // W
    for sb in range(n_sb):
        rows = slice(sb * W, (sb + 3) * W)
        edge = None
        if sb == 0 or sb == n_sb - 1:
            key_pos = j * tq + (sb - 1) * W + kj
            edge = jnp.where((key_pos >= 0) & (key_pos < seq), 0.0, -jnp.inf)
        scores = []
        for h in range(A_HEADS):
            pair, par = divmod(h, 2)
            cols = slice((h // G) * LANES, (h // G + 1) * LANES)
            qp = q_ref[0, sb * W:(sb + 1) * W, pair * LANES:(pair + 1) * LANES]
            scores.append(_dot_nt(qp, k_par[par][rows, cols]))
        probs, inv_den = [], []
        for h in range(A_HEADS):
            sink = sink_ref[h] * LOG2E
            s = scores[h] + bias[h]
            if edge is not None:
                s = s + edge
            m = jnp.maximum(jnp.max(s, axis=-1, keepdims=True), sink)
            ex = jnp.exp2(s - m)
            den = jnp.sum(ex, axis=-1, keepdims=True) + jnp.exp2(sink - m)
            probs.append(ex.astype(BF16))
            inv_den.append(1.0 / den)
        for pair in range(A_HEADS // 2):
            cols = slice(((2 * pair) // G) * LANES, ((2 * pair) // G + 1) * LANES)
            o = (_dot(probs[2 * pair], v_par[0][rows, cols]) * inv_den[2 * pair]
                 + _dot(probs[2 * pair + 1], v_par[1][rows, cols]) * inv_den[2 * pair + 1])
            o_ref[0, sb * W:(sb + 1) * W, pair * LANES:(pair + 1) * LANES] = o.astype(BF16)

    H = POOL_HALO
    ubuf[0:H] = jnp.where(j > 0, up_ref[0].astype(F32), 0.0)
    ubuf[H:H + tq] = um_ref[0].astype(F32)
    ubuf[H + tq:] = jnp.where(j < nj - 1, un_ref[0].astype(F32), 0.0)
    t = j * tq + lax.broadcasted_iota(jnp.int32, (tq, 1), 0)
    for g, w in enumerate(B_POOL_SIZES):
        half = w // 2
        cols = slice(g * B_GROUP_DIM, (g + 1) * B_GROUP_DIM)
        tot = jnp.zeros((tq, B_GROUP_DIM), F32)
        for d in range(-half, half):
            tot = tot + ubuf[H + d:H + d + tq, cols]
        cnt = (jnp.minimum(t + half, seq) - jnp.maximum(t - half, 0)).astype(F32)
        dlt = (tot / cnt - ubuf[H:H + tq, cols]).astype(BF16)
        y = _dot(dlt, pw_ref[g]) * ps_ref[:, cols]
        o_ref[0, :, A_Q_DIM + g * B_GROUP_DIM:A_Q_DIM + (g + 1) * B_GROUP_DIM] = y.astype(BF16)


def _win_pool(q, k, v, u, sink, pool_w, pool_scale, tq):
    B, S, _ = q.shape
    W, H = WIN_BLOCK, POOL_HALO
    nw, nh = S // W, S // H
    rw, rh = tq // W, tq // H
    main = lambda w_: pl.BlockSpec((1, tq, w_), lambda b, j: (b, j, 0))
    prev = lambda rows, r, w_: pl.BlockSpec((1, rows, w_), lambda b, j: (b, jnp.maximum(j * r - 1, 0), 0))
    nxt = lambda rows, r, n, w_: pl.BlockSpec((1, rows, w_), lambda b, j: (b, jnp.minimum((j + 1) * r, n - 1), 0))
    kernel = functools.partial(_win_pool_kernel, seq=S, tq=tq)
    return pl.pallas_call(
        kernel,
        grid=(B, S // tq),
        in_specs=[pl.BlockSpec(memory_space=pltpu.SMEM), main(512),
                  prev(W, rw, 256), main(256), nxt(W, rw, nw, 256),
                  prev(W, rw, 256), main(256), nxt(W, rw, nw, 256),
                  prev(H, rh, 512), main(512), nxt(H, rh, nh, 512),
                  _const_spec(pool_w.shape), _const_spec(pool_scale.shape)],
        out_specs=main(1024),
        out_shape=jax.ShapeDtypeStruct((B, S, 1024), BF16),
        scratch_shapes=[pltpu.VMEM((tq + 2 * W, 256), BF16), pltpu.VMEM((tq + 2 * W, 256), BF16),
                        pltpu.VMEM((tq + 2 * H, 512), F32)],
        compiler_params=_params(("parallel", "parallel")),
        name="win_pool",
    )(sink, q, k, k, k, v, v, v, u, u, u, pool_w, pool_scale)


def _out_ffn_ple_kernel(h_ref, o_ref, p_ref, wo_ref, nf_ref, wg_ref, wu_ref, wd_ref,
                        gn_ref, pg_ref, pp_ref, out_ref, *, ff_chunk):
    h1 = h_ref[0] + _dot(o_ref[0], wo_ref[...])
    xn = _rms(h1, nf_ref[...]).astype(BF16)
    acc = jnp.zeros_like(h1)
    for c0 in range(0, D_FF, ff_chunk):
        g = _dot(xn, wg_ref[:, c0:c0 + ff_chunk])
        u = _dot(xn, wu_ref[:, c0:c0 + ff_chunk])
        hdn = (g * jax.nn.sigmoid(g) * u).astype(BF16)
        acc = acc + _dot(hdn, wd_ref[c0:c0 + ff_chunk, :])
    out_ref[0] = _ple(h1 + acc, p_ref[0], gn_ref[...], pg_ref[...], pp_ref[...])


def _out_ffn_ple(h, o, p, wo, nf, wg, wu, wd, gn, pg, pp, tm):
    B, S, D = h.shape
    tok = lambda w_: pl.BlockSpec((1, tm, w_), lambda b, i: (b, i, 0))
    consts = [wo, nf, wg, wu, wd, gn, pg, pp]
    return pl.pallas_call(
        functools.partial(_out_ffn_ple_kernel, ff_chunk=D_FF // 2),
        grid=(B, S // tm),
        in_specs=[tok(D), tok(D), tok(PLE_DIM)] + [_const_spec(c.shape) for c in consts],
        out_specs=tok(D),
        out_shape=jax.ShapeDtypeStruct((B, S, D), F32),
        compiler_params=_params(("parallel", "parallel")),
        name="out_ffn_ple",
    )(h, o, p, *consts)


def _mla_in_kernel(h_ref, g_ref, w_ref, qln_ref, kln_ref, wuq_ref, gq_ref, cosT_ref, sinT_ref,
                   wuk_ref, wuv_ref, gkn_ref, gkr_ref, cos_ref, sin_ref, rot_ref, place_ref, e_ref,
                   qT_ref, k_ref, vT_ref):
    xn = _rms(h_ref[0], g_ref[...]).astype(BF16)
    a = _dot(xn, w_ref[...])
    cq = _rms(a[:, 0:C_Q_LORA], qln_ref[...]).astype(BF16)
    ckv = _rms(a[:, C_Q_LORA:C_Q_LORA + C_KV_LORA], kln_ref[...]).astype(BF16)
    kpe = a[:, C_Q_LORA + C_KV_LORA:]

    qT = _dot_nt(wuq_ref[...], cq)
    cosT = cosT_ref[...]
    sinT = sinT_ref[...]
    half = C_ROPE // 2
    for h in range(C_HEADS):
        blk = qT[h * LANES:(h + 1) * LANES]
        ss = jnp.sum(blk * blk, axis=0, keepdims=True)
        qn = blk * lax.rsqrt(ss * (1.0 / C_QK) + EPS) * gq_ref[h * LANES:(h + 1) * LANES]
        x1 = qn[C_NOPE:C_NOPE + half]
        x2 = qn[C_NOPE + half:C_QK]
        out = jnp.concatenate([qn[0:C_NOPE], x1 * cosT - x2 * sinT, x1 * sinT + x2 * cosT, qn[C_QK:]], axis=0)
        qT_ref[0, h] = out.astype(BF16)

    vT = _dot_nt(wuv_ref[...], ckv)
    ones = jnp.ones((ATT_ONES, vT.shape[1]), BF16)
    for h in range(C_HEADS):
        vT_ref[0, h, 0:C_V] = vT[h * C_V:(h + 1) * C_V].astype(BF16)
        vT_ref[0, h, C_V:] = ones

    kraw = _dot(ckv, wuk_ref[...])
    kg = kpe * gkr_ref[...]
    krope = kg * cos_ref[...] + _dot(kg.astype(BF16), rot_ref[...]) * sin_ref[...]
    placed = _dot(krope.astype(BF16), place_ref[...])
    ss_pe = jnp.sum(kpe * kpe, axis=-1, keepdims=True)
    e = e_ref[...]
    for h2 in range(C_HEADS // 2):
        cols = slice(h2 * 2 * LANES, (h2 + 1) * 2 * LANES)
        kr = kraw[:, cols]
        ss = _dot((kr * kr).astype(BF16), e) + ss_pe
        kn = (kr * gkn_ref[:, cols] + placed[:, cols]) * lax.rsqrt(ss * (1.0 / C_QK) + EPS)
        k_ref[0, 2 * h2] = kn[:, 0:LANES].astype(BF16)
        k_ref[0, 2 * h2 + 1] = kn[:, LANES:].astype(BF16)


def _mla_in(h, g, w, qln, kln, wuq, gq, cosT, sinT, wuk, wuv, gkn, gkr, cos, sin, rot, place, e, tm):
    B, S, D = h.shape
    consts_a = [g, w, qln, kln, wuq, gq]
    consts_b = [wuk, wuv, gkn, gkr]
    consts_c = [rot, place, e]
    return pl.pallas_call(
        _mla_in_kernel,
        grid=(B, S // tm),
        in_specs=[pl.BlockSpec((1, tm, D), lambda b, i: (b, i, 0))]
        + [_const_spec(c.shape) for c in consts_a]
        + [pl.BlockSpec((C_ROPE // 2, tm), lambda b, i: (0, i))] * 2
        + [_const_spec(c.shape) for c in consts_b]
        + [pl.BlockSpec((tm, LANES), lambda b, i: (i, 0))] * 2
        + [_const_spec(c.shape) for c in consts_c],
        out_specs=[pl.BlockSpec((1, C_HEADS, LANES, tm), lambda b, i: (b, 0, 0, i)),
                   pl.BlockSpec((1, C_HEADS, tm, LANES), lambda b, i: (b, 0, i, 0)),
                   pl.BlockSpec((1, C_HEADS, C_V + ATT_ONES, tm), lambda b, i: (b, 0, 0, i))],
        out_shape=[jax.ShapeDtypeStruct((B, C_HEADS, LANES, S), BF16),
                   jax.ShapeDtypeStruct((B, C_HEADS, S, LANES), BF16),
                   jax.ShapeDtypeStruct((B, C_HEADS, C_V + ATT_ONES, S), BF16)],
        compiler_params=_params(("parallel", "parallel")),
        name="mla_in",
    )(h, *consts_a, cosT, sinT, *consts_b, cos, sin, *consts_c)


ATT_STRIP = 256
ATT_SUB = 128
ATT_ONES = 16
ATT_AHEAD = 12
ATT_TQ = 1024
ATT_TK = 2048


def _mla_attn_kernel(qT_ref, k_ref, vT_ref, oT_ref, m_sc, acc_sc, *, tk):
    seq = k_ref.shape[2]
    tq = qT_ref.shape[3]
    m_sc[...] = jnp.full_like(m_sc, -jnp.inf)
    acc_sc[...] = jnp.zeros_like(acc_sc)

    items = [(st, j) for j in range(tk // ATT_SUB) for st in range(tq // ATT_STRIP)]

    def body(c, carry):
        def key_off(j):
            return pl.multiple_of(c * tk + j * ATT_SUB, ATT_SUB)

        def scores(item):
            st, j = item
            return _dot(k_ref[0, 0, pl.ds(key_off(j), ATT_SUB), :],
                        qT_ref[0, 0, :, st * ATT_STRIP:(st + 1) * ATT_STRIP])

        pending = [scores(it) for it in items[:ATT_AHEAD]]
        for idx, (st, j) in enumerate(items):
            if idx + ATT_AHEAD < len(items):
                pending.append(scores(items[idx + ATT_AHEAD]))
            s = pending.pop(0)
            cols = slice(st * ATT_STRIP, (st + 1) * ATT_STRIP)
            m = m_sc[:, cols]
            m_new = jnp.maximum(m, jnp.max(s, axis=0, keepdims=True))
            p = jnp.exp2(s - m_new).astype(BF16)
            v1 = vT_ref[0, 0, :, pl.ds(key_off(j), ATT_SUB)]
            acc_sc[:, cols] = jnp.exp2(m - m_new) * acc_sc[:, cols] + _dot(v1, p)
            m_sc[:, cols] = m_new
        return carry

    lax.fori_loop(0, seq // tk, body, 0)
    acc = acc_sc[...]
    oT_ref[0, 0] = (acc[0:C_V] / acc[C_V:C_V + 1]).astype(BF16)


def _mla_attn(qT, k, vT, tq, tk):
    B, H, _, S = qT.shape
    return pl.pallas_call(
        functools.partial(_mla_attn_kernel, tk=tk),
        grid=(B, H, S // tq),
        in_specs=[pl.BlockSpec((1, 1, LANES, tq), lambda b, h, i: (b, h, 0, i)),
                  pl.BlockSpec((1, 1, S, LANES), lambda b, h, i: (b, h, 0, 0)),
                  pl.BlockSpec((1, 1, C_V + ATT_ONES, S), lambda b, h, i: (b, h, 0, 0))],
        out_specs=pl.BlockSpec((1, 1, C_V, tq), lambda b, h, i: (b, h, 0, i)),
        out_shape=jax.ShapeDtypeStruct((B, H, C_V, S), BF16),
        scratch_shapes=[pltpu.VMEM((1, tq), F32), pltpu.VMEM((C_V + ATT_ONES, tq), F32)],
        compiler_params=_params(("parallel", "parallel", "arbitrary")),
        name="mla_attn",
    )(qT, k, vT)


def _mla_out_router_kernel(h_ref, oT_ref, wo_ref, nf_ref, r_ref, h1_ref, xn_ref, route_ref):
    h1 = h_ref[0] + _dot_tn(oT_ref[0], wo_ref[...])
    h1_ref[0] = h1
    xn = _rms(h1, nf_ref[...])
    hi = xn.astype(BF16)
    xn_ref[0] = hi
    lo = (xn - hi.astype(F32)).astype(BF16)
    p_hi = _dot(hi, r_ref[...])
    p_lo = _dot(lo, r_ref[...])
    logits = p_hi + p_lo + pltpu.roll(p_hi, LANES - MOE_EXPERTS, axis=1)
    lane = lax.broadcasted_iota(jnp.int32, logits.shape, 1)
    logits = jnp.where(lane < MOE_EXPERTS, logits, -jnp.inf)
    m1 = jnp.max(logits, axis=-1, keepdims=True)
    i1 = jnp.min(jnp.where(logits == m1, lane, LANES), axis=-1, keepdims=True)
    rest = jnp.where(lane == i1, -jnp.inf, logits)
    m2 = jnp.max(rest, axis=-1, keepdims=True)
    i2 = jnp.min(jnp.where(rest == m2, lane, LANES), axis=-1, keepdims=True)
    e2 = jnp.exp(m2 - m1)
    w1 = 1.0 / (1.0 + e2)
    w2 = e2 * w1
    route = jnp.where(lane == 0, i1.astype(F32),
                      jnp.where(lane == 1, i2.astype(F32),
                                jnp.where(lane == 2, w1, jnp.where(lane == 3, w2, 0.0))))
    route_ref[0] = route


def _mla_out_router(h, oT, wo, nf, r, tm):
    B, S, D = h.shape
    tok = lambda w_: pl.BlockSpec((1, tm, w_), lambda b, i: (b, i, 0))
    return pl.pallas_call(
        _mla_out_router_kernel,
        grid=(B, S // tm),
        in_specs=[tok(D), pl.BlockSpec((1, D, tm), lambda b, i: (b, 0, i)),
                  _const_spec(wo.shape), _const_spec(nf.shape), _const_spec(r.shape)],
        out_specs=[tok(D), tok(D), tok(LANES)],
        out_shape=[jax.ShapeDtypeStruct((B, S, D), F32), jax.ShapeDtypeStruct((B, S, D), BF16),
                   jax.ShapeDtypeStruct((B, S, LANES), F32)],
        compiler_params=_params(("parallel", "parallel")),
        name="mla_out_router",
    )(h, oT, wo, nf, r)


def _experts_kernel(te_ref, tv_ref, x_ref, wg_ref, wu_ref, wd_ref, y_ref, acc_sc):
    i = pl.program_id(0)
    f = pl.program_id(1)

    @pl.when(tv_ref[i] > 0)
    def _():
        x = x_ref[...]
        g = _dot(x, wg_ref[0])
        u = _dot(x, wu_ref[0])
        hdn = (g * jax.nn.sigmoid(g) * u).astype(BF16)
        y = _dot(hdn, wd_ref[0])

        @pl.when(f == 0)
        def _():
            acc_sc[...] = y

        @pl.when(f == pl.num_programs(1) - 1)
        def _():
            y_ref[...] = (acc_sc[...] + y).astype(BF16)


def _experts(xs, te, tv, wg, wu, wd, tm, nf):
    P, D = xs.shape
    fc = E_FF // nf
    grid_spec = pltpu.PrefetchScalarGridSpec(
        num_scalar_prefetch=2,
        grid=(P // tm, nf),
        in_specs=[pl.BlockSpec((tm, D), lambda i, f, te, tv: (i, 0)),
                  pl.BlockSpec((1, D, fc), lambda i, f, te, tv: (te[i], 0, f)),
                  pl.BlockSpec((1, D, fc), lambda i, f, te, tv: (te[i], 0, f)),
                  pl.BlockSpec((1, fc, D), lambda i, f, te, tv: (te[i], f, 0))],
        out_specs=pl.BlockSpec((tm, D), lambda i, f, te, tv: (i, 0)),
        scratch_shapes=[pltpu.VMEM((tm, D), F32)],
    )
    return pl.pallas_call(
        _experts_kernel,
        grid_spec=grid_spec,
        out_shape=jax.ShapeDtypeStruct((P, D), BF16),
        compiler_params=_params(("parallel", "arbitrary")),
        name="experts",
    )(te, tv, xs, wg, wu, wd)


def _combine_ple_kernel(h_ref, y0_ref, y1_ref, r_ref, p_ref, gn_ref, pg_ref, pp_ref, out_ref):
    r = r_ref[0]
    h2 = h_ref[0] + r[:, 2:3] * y0_ref[0].astype(F32) + r[:, 3:4] * y1_ref[0].astype(F32)
    out_ref[0] = _ple(h2, p_ref[0], gn_ref[...], pg_ref[...], pp_ref[...])


def _combine_ple(h, y0, y1, route, p, gn, pg, pp, tm):
    B, S, D = h.shape
    tok = lambda w_: pl.BlockSpec((1, tm, w_), lambda b, i: (b, i, 0))
    consts = [gn, pg, pp]
    return pl.pallas_call(
        _combine_ple_kernel,
        grid=(B, S // tm),
        in_specs=[tok(D), tok(D), tok(D), tok(LANES), tok(PLE_DIM)] + [_const_spec(c.shape) for c in consts],
        out_specs=tok(D),
        out_shape=jax.ShapeDtypeStruct((B, S, D), F32),
        compiler_params=_params(("parallel", "parallel")),
        name="combine_ple",
    )(h, y0, y1, route, p, *consts)


def _block_diag_ones(n, blk):
    i = np.arange(n)
    return jnp.asarray((i[:, None] // blk == i[None, :] // blk).astype(np.float32), BF16)


def _prep_even(W, j, i):
    w = W['ab_w_in'][j]
    c = lambda a, b: w[:, a:b]
    k0, k1 = c(512, 576), c(576, 640)
    v0, v1 = c(640, 704), c(704, 768)
    w_in = jnp.concatenate([c(0, 512), k0, k0, k1, k1, v0, v0, v1, v1, c(768, 1280)], axis=1).astype(BF16)
    return dict(
        g=W['norm_mix'][i][None], w_in=w_in, e=_block_diag_ones(256, A_HEAD_DIM),
        qg=(jnp.tile(W['ab_q_norm'][j], A_HEADS) * (A_HEAD_DIM ** -0.5 * LOG2E))[None],
        kg=jnp.tile(W['ab_k_norm'][j], 2 * A_KV_HEADS)[None],
        sink=W['ab_sink'][j], pool_w=W['ab_pool_w'][j].astype(BF16), pool_scale=W['ab_pool_scale'][j][None],
        wo=W['ab_w_out'][j].astype(BF16), nf=W['norm_ffn'][i][None],
        wg=W['ffn_w_gate'][j].astype(BF16), wu=W['ffn_w_up'][j].astype(BF16), wd=W['ffn_w_down'][j].astype(BF16),
        gn=W['ple_gate_norm'][i][None], pg=W['ple_w_gate'][i].astype(BF16), pp=W['ple_w_proj'][i].astype(BF16))


def _prep_odd(W, j, i):
    pad_h = lambda a, n: jnp.pad(a, [(0, 0)] * (a.ndim - 1) + [(0, n - a.shape[-1])])
    w_in = pad_h(W['mla_w_in'][j], 512).astype(BF16)
    wuq = pad_h(W['mla_w_uq'][j].reshape(C_Q_LORA, C_HEADS, C_QK), LANES).reshape(C_Q_LORA, C_HEADS * LANES)
    gq = jnp.tile(pad_h(W['mla_q_norm'][j], LANES), C_HEADS) * (C_QK ** -0.5 * math.log2(math.e))
    ukv = W['mla_w_ukv'][j].reshape(C_KV_LORA, C_HEADS, C_NOPE + C_V)
    wuk = pad_h(ukv[..., :C_NOPE], LANES).reshape(C_KV_LORA, C_HEADS * LANES)
    wuv = ukv[..., C_NOPE:].reshape(C_KV_LORA, C_HEADS * C_V)
    kn = W['mla_k_norm'][j]
    gkn = jnp.tile(pad_h(kn[:C_NOPE], LANES), C_HEADS)[None]
    gkr = pad_h(kn[C_NOPE:], LANES)[None]
    half = C_ROPE // 2
    rot = np.zeros((LANES, LANES), np.float32)
    rot[np.arange(half) + half, np.arange(half)] = -1.0
    rot[np.arange(half), np.arange(half) + half] = 1.0
    place = np.zeros((LANES, C_HEADS * LANES), np.float32)
    for h in range(C_HEADS):
        place[np.arange(C_ROPE), h * LANES + C_NOPE + np.arange(C_ROPE)] = 1.0
    router = W['moe_router'][j]
    r_hi = router.astype(BF16)
    r_lo = (router - r_hi.astype(F32)).astype(BF16)
    r = pad_h(jnp.concatenate([r_hi, r_lo], axis=1), LANES)
    return dict(
        g=W['norm_mix'][i][None], w_in=w_in, qln=W['mla_q_lat_norm'][j][None], kln=W['mla_kv_lat_norm'][j][None],
        wuq=wuq.T.astype(BF16), gq=gq[:, None], wuk=wuk.astype(BF16), wuv=wuv.T.astype(BF16), gkn=gkn, gkr=gkr,
        rot=jnp.asarray(rot, BF16), place=jnp.asarray(place, BF16), e=_block_diag_ones(2 * LANES, LANES),
        wo=W['mla_w_out'][j].astype(BF16), nf=W['norm_ffn'][i][None], r=r,
        wg=W['moe_w_gate'][j].astype(BF16), wu=W['moe_w_up'][j].astype(BF16), wd=W['moe_w_down'][j].astype(BF16),
        gn=W['ple_gate_norm'][i][None], pg=W['ple_w_gate'][i].astype(BF16), pp=W['ple_w_proj'][i].astype(BF16))


def _rope_tables(S):
    half = C_ROPE // 2
    inv = ROPE_THETA ** (-jnp.arange(half, dtype=F32) / half)
    ang = jnp.arange(S).astype(F32)[:, None] * inv[None, :]
    cos, sin = jnp.cos(ang), jnp.sin(ang)
    wide = lambda t: jnp.pad(jnp.concatenate([t, t], axis=1), ((0, 0), (0, LANES - C_ROPE)))
    return cos.T, sin.T, wide(cos), wide(sin)


def _route_layout(ids, tm):
    n_slots = ids.size
    n_tiles = n_slots // tm + MOE_EXPERTS
    flat = ids.reshape(-1)
    order = jnp.argsort(flat, stable=True).astype(jnp.int32)
    rank = jnp.argsort(order).astype(jnp.int32)
    slot_end = jnp.searchsorted(flat[order], jnp.arange(MOE_EXPERTS, dtype=flat.dtype), side='right')
    slot_end = slot_end.astype(jnp.int32)
    counts = slot_end - jnp.concatenate([jnp.zeros((1,), jnp.int32), slot_end[:-1]])
    slot_start = slot_end - counts
    tiles = (counts + tm - 1) // tm
    tile_end = jnp.cumsum(tiles)
    tile_start = tile_end - tiles
    te = jnp.sum(jnp.arange(n_tiles)[:, None] >= tile_end[None, :], axis=1)
    tv = (te < MOE_EXPERTS).astype(jnp.int32)
    te = jnp.minimum(te, MOE_EXPERTS - 1).astype(jnp.int32)
    base = tile_start * tm - slot_start
    q0 = jnp.arange(n_tiles, dtype=jnp.int32) * tm - base[te]
    q = q0[:, None] + jnp.arange(tm, dtype=jnp.int32)[None, :]
    valid = (tv[:, None] > 0) & (q < slot_end[te][:, None])
    src = jnp.where(valid, order[jnp.clip(q, 0, n_slots - 1)] // 2, 0).reshape(-1)
    row_of_slot = rank + base[flat]
    return src, te, tv, row_of_slot.reshape(ids.shape)


def _tiles(S):
    return dict(tm=min(512, S), tq_win=min(512, S), tq=min(ATT_TQ, S), tk=min(ATT_TK, S), te=512, nf=2)


def _trunk(x, p, even, odd):
    B, S, D = x.shape
    t = _tiles(S)
    q, k, v, u = _ab_in(x, even['g'], even['w_in'], even['e'], even['qg'], even['kg'], t['tm'])
    o = _win_pool(q, k, v, u, even['sink'], even['pool_w'], even['pool_scale'], t['tq_win'])
    h = _out_ffn_ple(x, o, p[0], even['wo'], even['nf'], even['wg'], even['wu'], even['wd'],
                     even['gn'], even['pg'], even['pp'], t['tm'])
    cosT, sinT, cos, sin = _rope_tables(S)
    qT, kk, vT = _mla_in(h, odd['g'], odd['w_in'], odd['qln'], odd['kln'], odd['wuq'], odd['gq'], cosT, sinT,
                         odd['wuk'], odd['wuv'], odd['gkn'], odd['gkr'], cos, sin, odd['rot'], odd['place'],
                         odd['e'], t['tm'])
    oT = _mla_attn(qT, kk, vT, t['tq'], t['tk'])
    h1, xn, route = _mla_out_router(h, oT.reshape(B, C_HEADS * C_V, S), odd['wo'], odd['nf'], odd['r'], t['tm'])
    ids = route[..., 0:2].astype(jnp.int32).reshape(B * S, 2)
    src, te, tv, rows = _route_layout(ids, t['te'])
    xs = jnp.take(xn.reshape(B * S, D), src, axis=0)
    ys = _experts(xs, te, tv, odd['wg'], odd['wu'], odd['wd'], t['te'], t['nf'])
    y0 = jnp.take(ys, rows[:, 0], axis=0).reshape(B, S, D)
    y1 = jnp.take(ys, rows[:, 1], axis=0).reshape(B, S, D)
    return _combine_ple(h1, y0, y1, route, p[1], odd['gn'], odd['pg'], odd['pp'], t['tm'])


def kernel(x_prompt, x_sample, p_prompt, p_sample, norm_mix, norm_ffn, ab_w_in, ab_q_norm, ab_k_norm, ab_sink, ab_pool_w, ab_pool_scale, ab_w_out, ffn_w_gate, ffn_w_up, ffn_w_down, mla_w_in, mla_q_lat_norm, mla_kv_lat_norm, mla_w_uq, mla_w_ukv, mla_q_norm, mla_k_norm, mla_w_out, moe_router, moe_w_gate, moe_w_up, moe_w_down, ple_w_proj, ple_gate_norm, ple_w_gate):
    W = dict(norm_mix=norm_mix, norm_ffn=norm_ffn,
             ab_w_in=ab_w_in, ab_q_norm=ab_q_norm, ab_k_norm=ab_k_norm, ab_sink=ab_sink,
             ab_pool_w=ab_pool_w, ab_pool_scale=ab_pool_scale, ab_w_out=ab_w_out,
             ffn_w_gate=ffn_w_gate, ffn_w_up=ffn_w_up, ffn_w_down=ffn_w_down,
             mla_w_in=mla_w_in, mla_q_lat_norm=mla_q_lat_norm, mla_kv_lat_norm=mla_kv_lat_norm,
             mla_w_uq=mla_w_uq, mla_w_ukv=mla_w_ukv, mla_q_norm=mla_q_norm, mla_k_norm=mla_k_norm,
             mla_w_out=mla_w_out, moe_router=moe_router, moe_w_gate=moe_w_gate, moe_w_up=moe_w_up,
             moe_w_down=moe_w_down, ple_w_proj=ple_w_proj, ple_gate_norm=ple_gate_norm,
             ple_w_gate=ple_w_gate)
    even = _prep_even(W, 0, 0)
    odd = _prep_odd(W, 0, 1)
    return (_trunk(x_prompt, p_prompt, even, odd), _trunk(x_sample, p_sample, even, odd))
```

```python
import functools
import math

import numpy as np
import jax
import jax.numpy as jnp
from jax import lax
from jax.experimental import pallas as pl
from jax.experimental.pallas import tpu as pltpu

F32 = jnp.float32
BF16 = jnp.bfloat16

D_MODEL = 1024
EPS = 1e-6
LOG2E = math.log2(math.e)
LANES = 128
VMEM_LIMIT = 56 * 1024 * 1024

A_HEADS, A_KV_HEADS, A_HEAD_DIM, A_WINDOW = 8, 2, 64, 128
A_Q_DIM = A_HEADS * A_HEAD_DIM
B_WIDTH, B_GROUPS, B_GROUP_DIM = 512, 4, 128
B_POOL_SIZES = (2, 4, 8, 16)
POOL_HALO = 16
WIN_BLOCK = 128
C_HEADS, C_NOPE, C_ROPE, C_V = 16, 64, 32, 64
C_QK = C_NOPE + C_ROPE
C_Q_LORA, C_KV_LORA = 256, 128
ROPE_THETA = 10000.0
D_FF, MOE_EXPERTS, E_FF = 2816, 8, 3584
PLE_DIM = 256


def _const_spec(shape):
    nd = len(shape)
    return pl.BlockSpec(shape, lambda *_: (0,) * nd, pipeline_mode=pl.Buffered(1))


def _params(sem):
    return pltpu.CompilerParams(dimension_semantics=sem, vmem_limit_bytes=VMEM_LIMIT)


def _rms(x, g):
    ms = jnp.mean(x * x, axis=-1, keepdims=True)
    return x * lax.rsqrt(ms + EPS) * g


def _dot(a, b):
    return jnp.dot(a, b, preferred_element_type=F32)


def _dot_nt(a, b):
    return lax.dot_general(a, b, (((1,), (1,)), ((), ())), preferred_element_type=F32)


def _dot_tn(a, b):
    return lax.dot_general(a, b, (((0,), (0,)), ((), ())), preferred_element_type=F32)


def _ple(h, p, gnorm, w_gate, w_proj):
    gate = jax.nn.sigmoid(_dot(_rms(h, gnorm).astype(BF16), w_gate))
    return h + gate * _dot(p.astype(BF16), w_proj)


def _ab_in_kernel(h_ref, g_ref, w_ref, e_ref, qg_ref, kg_ref, q_ref, k_ref, v_ref, u_ref):
    xn = _rms(h_ref[0], g_ref[...]).astype(BF16)
    a = _dot(xn, w_ref[...])
    e = e_ref[...]

    def head_norm(t, g):
        ss = _dot((t * t).astype(BF16), e)
        return t * lax.rsqrt(ss * (1.0 / A_HEAD_DIM) + EPS) * g

    q_ref[0, :, 0:256] = head_norm(a[:, 0:256], qg_ref[:, 0:256]).astype(BF16)
    q_ref[0, :, 256:512] = head_norm(a[:, 256:512], qg_ref[:, 256:512]).astype(BF16)
    k_ref[0] = head_norm(a[:, 512:768], kg_ref[...]).astype(BF16)
    v_ref[0] = a[:, 768:1024].astype(BF16)
    u_ref[0] = a[:, 1024:1536].astype(BF16)


def _ab_in(h, g, w, e, qg, kg, tm):
    B, S, D = h.shape
    tok = lambda w_: pl.BlockSpec((1, tm, w_), lambda b, i: (b, i, 0))
    return pl.pallas_call(
        _ab_in_kernel,
        grid=(B, S // tm),
        in_specs=[tok(D), _const_spec(g.shape), _const_spec(w.shape), _const_spec(e.shape),
                  _const_spec(qg.shape), _const_spec(kg.shape)],
        out_specs=[tok(512), tok(256), tok(256), tok(512)],
        out_shape=[jax.ShapeDtypeStruct((B, S, 512), BF16), jax.ShapeDtypeStruct((B, S, 256), BF16),
                   jax.ShapeDtypeStruct((B, S, 256), BF16), jax.ShapeDtypeStruct((B, S, 512), BF16)],
        compiler_params=_params(("parallel", "parallel")),
        name="ab_in",
    )(h, g, w, e, qg, kg)


def _win_pool_kernel(sink_ref, q_ref, kp_ref, km_ref, kn_ref, vp_ref, vm_ref, vn_ref,
                     up_ref, um_ref, un_ref, pw_ref, ps_ref, o_ref, kbuf, vbuf, ubuf, *, seq, tq):
    j = pl.program_id(1)
    nj = pl.num_programs(1)
    W = WIN_BLOCK
    kbuf[0:W] = kp_ref[0]
    kbuf[W:W + tq] = km_ref[0]
    kbuf[W + tq:] = kn_ref[0]
    vbuf[0:W] = vp_ref[0]
    vbuf[W:W + tq] = vm_ref[0]
    vbuf[W + tq:] = vn_ref[0]

    lane = lax.broadcasted_iota(jnp.int32, (1, 2 * LANES), 1) % LANES
    lo = lane < A_HEAD_DIM
    kb = kbuf[...]
    vb = vbuf[...]
    zero = jnp.zeros_like(kb)
    k_par = (jnp.where(lo, kb, zero), jnp.where(lo, zero, kb))
    v_par = (jnp.where(lo, vb, zero), jnp.where(lo, zero, vb))

    qi = lax.broadcasted_iota(jnp.int32, (W, 3 * W), 0)
    kj = lax.broadcasted_iota(jnp.int32, (W, 3 * W), 1)
    dist = jnp.abs(qi - kj + W)
    distf = dist.astype(F32)
    in_band = dist <= A_WINDOW
    G = A_HEADS // A_KV_HEADS
    bias = [jnp.where(in_band, (-LOG2E * 2.0 ** (-8.0 * (h + 1) / A_HEADS)) * distf, -jnp.inf)
            for h in range(A_HEADS)]
    n_sb = tq // W
    for sb in range(n_sb):
        rows = slice(sb * W, (sb + 3) * W)
        edge = None
        if sb == 0 or sb == n_sb - 1:
            key_pos = j * tq + (sb - 1) * W + kj
            edge = jnp.where((key_pos >= 0) & (key_pos < seq), 0.0, -jnp.inf)
        scores = []
        for h in range(A_HEADS):
            pair, par = divmod(h, 2)
            cols = slice((h // G) * LANES, (h // G + 1) * LANES)
            qp = q_ref[0, sb * W:(sb + 1) * W, pair * LANES:(pair + 1) * LANES]
            scores.append(_dot_nt(qp, k_par[par][rows, cols]))
        probs, inv_den = [], []
        for h in range(A_HEADS):
            sink = sink_ref[h] * LOG2E
            s = scores[h] + bias[h]
            if edge is not None:
                s = s + edge
            m = jnp.maximum(jnp.max(s, axis=-1, keepdims=True), sink)
            ex = jnp.exp2(s - m)
            den = jnp.sum(ex, axis=-1, keepdims=True) + jnp.exp2(sink - m)
            probs.append(ex.astype(BF16))
            inv_den.append(1.0 / den)
        for pair in range(A_HEADS // 2):
            cols = slice(((2 * pair) // G) * LANES, ((2 * pair) // G + 1) * LANES)
            o = (_dot(probs[2 * pair], v_par[0][rows, cols]) * inv_den[2 * pair]
                 + _dot(probs[2 * pair + 1], v_par[1][rows, cols]) * inv_den[2 * pair + 1])
            o_ref[0, sb * W:(sb + 1) * W, pair * LANES:(pair + 1) * LANES] = o.astype(BF16)

    H = POOL_HALO
    ubuf[0:H] = jnp.where(j > 0, up_ref[0].astype(F32), 0.0)
    ubuf[H:H + tq] = um_ref[0].astype(F32)
    ubuf[H + tq:] = jnp.where(j < nj - 1, un_ref[0].astype(F32), 0.0)
    t = j * tq + lax.broadcasted_iota(jnp.int32, (tq, 1), 0)
    for g, w in enumerate(B_POOL_SIZES):
        half = w // 2
        cols = slice(g * B_GROUP_DIM, (g + 1) * B_GROUP_DIM)
        tot = jnp.zeros((tq, B_GROUP_DIM), F32)
        for d in range(-half, half):
            tot = tot + ubuf[H + d:H + d + tq, cols]
        cnt = (jnp.minimum(t + half, seq) - jnp.maximum(t - half, 0)).astype(F32)
        dlt = (tot / cnt - ubuf[H:H + tq, cols]).astype(BF16)
        y = _dot(dlt, pw_ref[g]) * ps_ref[:, cols]
        o_ref[0, :, A_Q_DIM + g * B_GROUP_DIM:A_Q_DIM + (g + 1) * B_GROUP_DIM] = y.astype(BF16)


def _win_pool(q, k, v, u, sink, pool_w, pool_scale, tq):
    B, S, _ = q.shape
    W, H = WIN_BLOCK, POOL_HALO
    nw, nh = S // W, S // H
    rw, rh = tq // W, tq // H
    main = lambda w_: pl.BlockSpec((1, tq, w_), lambda b, j: (b, j, 0))
    prev = lambda rows, r, w_: pl.BlockSpec((1, rows, w_), lambda b, j: (b, jnp.maximum(j * r - 1, 0), 0))
    nxt = lambda rows, r, n, w_: pl.BlockSpec((1, rows, w_), lambda b, j: (b, jnp.minimum((j + 1) * r, n - 1), 0))
    kernel = functools.partial(_win_pool_kernel, seq=S, tq=tq)
    return pl.pallas_call(
        kernel,
        grid=(B, S // tq),
        in_specs=[pl.BlockSpec(memory_space=pltpu.SMEM), main(512),
                  prev(W, rw, 256), main(256), nxt(W, rw, nw, 256),
                  prev(W, rw, 256), main(256), nxt(W, rw, nw, 256),
                  prev(H, rh, 512), main(512), nxt(H, rh, nh, 512),
                  _const_spec(pool_w.shape), _const_spec(pool_scale.shape)],
        out_specs=main(1024),
        out_shape=jax.ShapeDtypeStruct((B, S, 1024), BF16),
        scratch_shapes=[pltpu.VMEM((tq + 2 * W, 256), BF16), pltpu.VMEM((tq + 2 * W, 256), BF16),
                        pltpu.VMEM((tq + 2 * H, 512), F32)],
        compiler_params=_params(("parallel", "parallel")),
        name="win_pool",
    )(sink, q, k, k, k, v, v, v, u, u, u, pool_w, pool_scale)


def _out_ffn_ple_kernel(h_ref, o_ref, p_ref, wo_ref, nf_ref, wg_ref, wu_ref, wd_ref,
                        gn_ref, pg_ref, pp_ref, out_ref, *, ff_chunk):
    h1 = h_ref[0] + _dot(o_ref[0], wo_ref[...])
    xn = _rms(h1, nf_ref[...]).astype(BF16)
    acc = jnp.zeros_like(h1)
    for c0 in range(0, D_FF, ff_chunk):
        g = _dot(xn, wg_ref[:, c0:c0 + ff_chunk])
        u = _dot(xn, wu_ref[:, c0:c0 + ff_chunk])
        hdn = (g * jax.nn.sigmoid(g) * u).astype(BF16)
        acc = acc + _dot(hdn, wd_ref[c0:c0 + ff_chunk, :])
    out_ref[0] = _ple(h1 + acc, p_ref[0], gn_ref[...], pg_ref[...], pp_ref[...])


def _out_ffn_ple(h, o, p, wo, nf, wg, wu, wd, gn, pg, pp, tm):
    B, S, D = h.shape
    tok = lambda w_: pl.BlockSpec((1, tm, w_), lambda b, i: (b, i, 0))
    consts = [wo, nf, wg, wu, wd, gn, pg, pp]
    return pl.pallas_call(
        functools.partial(_out_ffn_ple_kernel, ff_chunk=D_FF // 2),
        grid=(B, S // tm),
        in_specs=[tok(D), tok(D), tok(PLE_DIM)] + [_const_spec(c.shape) for c in consts],
        out_specs=tok(D),
        out_shape=jax.ShapeDtypeStruct((B, S, D), F32),
        compiler_params=_params(("parallel", "parallel")),
        name="out_ffn_ple",
    )(h, o, p, *consts)


def _mla_in_kernel(h_ref, g_ref, w_ref, qln_ref, kln_ref, wuq_ref, gq_ref, cosT_ref, sinT_ref,
                   wuk_ref, wuv_ref, gkn_ref, gkr_ref, cos_ref, sin_ref, rot_ref, place_ref, e_ref,
                   qT_ref, k_ref, vT_ref):
    xn = _rms(h_ref[0], g_ref[...]).astype(BF16)
    a = _dot(xn, w_ref[...])
    cq = _rms(a[:, 0:C_Q_LORA], qln_ref[...]).astype(BF16)
    ckv = _rms(a[:, C_Q_LORA:C_Q_LORA + C_KV_LORA], kln_ref[...]).astype(BF16)
    kpe = a[:, C_Q_LORA + C_KV_LORA:]

    qT = _dot_nt(wuq_ref[...], cq)
    cosT = cosT_ref[...]
    sinT = sinT_ref[...]
    half = C_ROPE // 2
    for h in range(C_HEADS):
        blk = qT[h * LANES:(h + 1) * LANES]
        ss = jnp.sum(blk * blk, axis=0, keepdims=True)
        qn = blk * lax.rsqrt(ss * (1.0 / C_QK) + EPS) * gq_ref[h * LANES:(h + 1) * LANES]
        x1 = qn[C_NOPE:C_NOPE + half]
        x2 = qn[C_NOPE + half:C_QK]
        out = jnp.concatenate([qn[0:C_NOPE], x1 * cosT - x2 * sinT, x1 * sinT + x2 * cosT, qn[C_QK:]], axis=0)
        qT_ref[0, h] = out.astype(BF16)

    vT = _dot_nt(wuv_ref[...], ckv)
    ones = jnp.ones((ATT_ONES, vT.shape[1]), BF16)
    for h in range(C_HEADS):
        vT_ref[0, h, 0:C_V] = vT[h * C_V:(h + 1) * C_V].astype(BF16)
        vT_ref[0, h, C_V:] = ones

    kraw = _dot(ckv, wuk_ref[...])
    kg = kpe * gkr_ref[...]
    krope = kg * cos_ref[...] + _dot(kg.astype(BF16), rot_ref[...]) * sin_ref[...]
    placed = _dot(krope.astype(BF16), place_ref[...])
    ss_pe = jnp.sum(kpe * kpe, axis=-1, keepdims=True)
    e = e_ref[...]
    for h2 in range(C_HEADS // 2):
        cols = slice(h2 * 2 * LANES, (h2 + 1) * 2 * LANES)
        kr = kraw[:, cols]
        ss = _dot((kr * kr).astype(BF16), e) + ss_pe
        kn = (kr * gkn_ref[:, cols] + placed[:, cols]) * lax.rsqrt(ss * (1.0 / C_QK) + EPS)
        k_ref[0, 2 * h2] = kn[:, 0:LANES].astype(BF16)
        k_ref[0, 2 * h2 + 1] = kn[:, LANES:].astype(BF16)


def _mla_in(h, g, w, qln, kln, wuq, gq, cosT, sinT, wuk, wuv, gkn, gkr, cos, sin, rot, place, e, tm):
    B, S, D = h.shape
    consts_a = [g, w, qln, kln, wuq, gq]
    consts_b = [wuk, wuv, gkn, gkr]
    consts_c = [rot, place, e]
    return pl.pallas_call(
        _mla_in_kernel,
        grid=(B, S // tm),
        in_specs=[pl.BlockSpec((1, tm, D), lambda b, i: (b, i, 0))]
        + [_const_spec(c.shape) for c in consts_a]
        + [pl.BlockSpec((C_ROPE // 2, tm), lambda b, i: (0, i))] * 2
        + [_const_spec(c.shape) for c in consts_b]
        + [pl.BlockSpec((tm, LANES), lambda b, i: (i, 0))] * 2
        + [_const_spec(c.shape) for c in consts_c],
        out_specs=[pl.BlockSpec((1, C_HEADS, LANES, tm), lambda b, i: (b, 0, 0, i)),
                   pl.BlockSpec((1, C_HEADS, tm, LANES), lambda b, i: (b, 0, i, 0)),
                   pl.BlockSpec((1, C_HEADS, C_V + ATT_ONES, tm), lambda b, i: (b, 0, 0, i))],
        out_shape=[jax.ShapeDtypeStruct((B, C_HEADS, LANES, S), BF16),
                   jax.ShapeDtypeStruct((B, C_HEADS, S, LANES), BF16),
                   jax.ShapeDtypeStruct((B, C_HEADS, C_V + ATT_ONES, S), BF16)],
        compiler_params=_params(("parallel", "parallel")),
        name="mla_in",
    )(h, *consts_a, cosT, sinT, *consts_b, cos, sin, *consts_c)


ATT_STRIP = 256
ATT_SUB = 128
ATT_ONES = 16
ATT_AHEAD = 12
ATT_TQ = 1024
ATT_TK = 2048


def _mla_attn_kernel(qT_ref, k_ref, vT_ref, oT_ref, m_sc, acc_sc, *, tk):
    seq = k_ref.shape[2]
    tq = qT_ref.shape[3]
    m_sc[...] = jnp.full_like(m_sc, -jnp.inf)
    acc_sc[...] = jnp.zeros_like(acc_sc)

    items = [(st, j) for j in range(tk // ATT_SUB) for st in range(tq // ATT_STRIP)]

    def body(c, carry):
        def key_off(j):
            return pl.multiple_of(c * tk + j * ATT_SUB, ATT_SUB)

        def scores(item):
            st, j = item
            return _dot(k_ref[0, 0, pl.ds(key_off(j), ATT_SUB), :],
                        qT_ref[0, 0, :, st * ATT_STRIP:(st + 1) * ATT_STRIP])

        pending = [scores(it) for it in items[:ATT_AHEAD]]
        for idx, (st, j) in enumerate(items):
            if idx + ATT_AHEAD < len(items):
                pending.append(scores(items[idx + ATT_AHEAD]))
            s = pending.pop(0)
            cols = slice(st * ATT_STRIP, (st + 1) * ATT_STRIP)
            m = m_sc[:, cols]
            m_new = jnp.maximum(m, jnp.max(s, axis=0, keepdims=True))
            p = jnp.exp2(s - m_new).astype(BF16)
            v1 = vT_ref[0, 0, :, pl.ds(key_off(j), ATT_SUB)]
            acc_sc[:, cols] = jnp.exp2(m - m_new) * acc_sc[:, cols] + _dot(v1, p)
            m_sc[:, cols] = m_new
        return carry

    lax.fori_loop(0, seq // tk, body, 0)
    acc = acc_sc[...]
    oT_ref[0, 0] = (acc[0:C_V] / acc[C_V:C_V + 1]).astype(BF16)


def _mla_attn(qT, k, vT, tq, tk):
    B, H, _, S = qT.shape
    return pl.pallas_call(
        functools.partial(_mla_attn_kernel, tk=tk),
        grid=(B, H, S // tq),
        in_specs=[pl.BlockSpec((1, 1, LANES, tq), lambda b, h, i: (b, h, 0, i)),
                  pl.BlockSpec((1, 1, S, LANES), lambda b, h, i: (b, h, 0, 0)),
                  pl.BlockSpec((1, 1, C_V + ATT_ONES, S), lambda b, h, i: (b, h, 0, 0))],
        out_specs=pl.BlockSpec((1, 1, C_V, tq), lambda b, h, i: (b, h, 0, i)),
        out_shape=jax.ShapeDtypeStruct((B, H, C_V, S), BF16),
        scratch_shapes=[pltpu.VMEM((1, tq), F32), pltpu.VMEM((C_V + ATT_ONES, tq), F32)],
        compiler_params=_params(("parallel", "parallel", "arbitrary")),
        name="mla_attn",
    )(qT, k, vT)


def _mla_out_router_kernel(h_ref, oT_ref, wo_ref, nf_ref, r_ref, h1_ref, xn_ref, route_ref):
    h1 = h_ref[0] + _dot_tn(oT_ref[0], wo_ref[...])
    h1_ref[0] = h1
    xn = _rms(h1, nf_ref[...])
    hi = xn.astype(BF16)
    xn_ref[0] = hi
    lo = (xn - hi.astype(F32)).astype(BF16)
    p_hi = _dot(hi, r_ref[...])
    p_lo = _dot(lo, r_ref[...])
    logits = p_hi + p_lo + pltpu.roll(p_hi, LANES - MOE_EXPERTS, axis=1)
    lane = lax.broadcasted_iota(jnp.int32, logits.shape, 1)
    logits = jnp.where(lane < MOE_EXPERTS, logits, -jnp.inf)
    m1 = jnp.max(logits, axis=-1, keepdims=True)
    i1 = jnp.min(jnp.where(logits == m1, lane, LANES), axis=-1, keepdims=True)
    rest = jnp.where(lane == i1, -jnp.inf, logits)
    m2 = jnp.max(rest, axis=-1, keepdims=True)
    i2 = jnp.min(jnp.where(rest == m2, lane, LANES), axis=-1, keepdims=True)
    e2 = jnp.exp(m2 - m1)
    w1 = 1.0 / (1.0 + e2)
    w2 = e2 * w1
    route = jnp.where(lane == 0, i1.astype(F32),
                      jnp.where(lane == 1, i2.astype(F32),
                                jnp.where(lane == 2, w1, jnp.where(lane == 3, w2, 0.0))))
    route_ref[0] = route


def _mla_out_router(h, oT, wo, nf, r, tm):
    B, S, D = h.shape
    tok = lambda w_: pl.BlockSpec((1, tm, w_), lambda b, i: (b, i, 0))
    return pl.pallas_call(
        _mla_out_router_kernel,
        grid=(B, S // tm),
        in_specs=[tok(D), pl.BlockSpec((1, D, tm), lambda b, i: (b, 0, i)),
                  _const_spec(wo.shape), _const_spec(nf.shape), _const_spec(r.shape)],
        out_specs=[tok(D), tok(D), tok(LANES)],
        out_shape=[jax.ShapeDtypeStruct((B, S, D), F32), jax.ShapeDtypeStruct((B, S, D), BF16),
                   jax.ShapeDtypeStruct((B, S, LANES), F32)],
        compiler_params=_params(("parallel", "parallel")),
        name="mla_out_router",
    )(h, oT, wo, nf, r)


def _experts_kernel(te_ref, tv_ref, x_ref, wg_ref, wu_ref, wd_ref, y_ref, acc_sc):
    i = pl.program_id(0)
    f = pl.program_id(1)

    @pl.when(tv_ref[i] > 0)
    def _():
        x = x_ref[...]
        g = _dot(x, wg_ref[0])
        u = _dot(x, wu_ref[0])
        hdn = (g * jax.nn.sigmoid(g) * u).astype(BF16)
        y = _dot(hdn, wd_ref[0])

        @pl.when(f == 0)
        def _():
            acc_sc[...] = y

        @pl.when(f == pl.num_programs(1) - 1)
        def _():
            y_ref[...] = (acc_sc[...] + y).astype(BF16)


def _experts(xs, te, tv, wg, wu, wd, tm, nf):
    P, D = xs.shape
    fc = E_FF // nf
    grid_spec = pltpu.PrefetchScalarGridSpec(
        num_scalar_prefetch=2,
        grid=(P // tm, nf),
        in_specs=[pl.BlockSpec((tm, D), lambda i, f, te, tv: (i, 0)),
                  pl.BlockSpec((1, D, fc), lambda i, f, te, tv: (te[i], 0, f)),
                  pl.BlockSpec((1, D, fc), lambda i, f, te, tv: (te[i], 0, f)),
                  pl.BlockSpec((1, fc, D), lambda i, f, te, tv: (te[i], f, 0))],
        out_specs=pl.BlockSpec((tm, D), lambda i, f, te, tv: (i, 0)),
        scratch_shapes=[pltpu.VMEM((tm, D), F32)],
    )
    return pl.pallas_call(
        _experts_kernel,
        grid_spec=grid_spec,
        out_shape=jax.ShapeDtypeStruct((P, D), BF16),
        compiler_params=_params(("parallel", "arbitrary")),
        name="experts",
    )(te, tv, xs, wg, wu, wd)


def _combine_ple_kernel(h_ref, y0_ref, y1_ref, r_ref, p_ref, gn_ref, pg_ref, pp_ref, out_ref):
    r = r_ref[0]
    h2 = h_ref[0] + r[:, 2:3] * y0_ref[0].astype(F32) + r[:, 3:4] * y1_ref[0].astype(F32)
    out_ref[0] = _ple(h2, p_ref[0], gn_ref[...], pg_ref[...], pp_ref[...])


def _combine_ple(h, y0, y1, route, p, gn, pg, pp, tm):
    B, S, D = h.shape
    tok = lambda w_: pl.BlockSpec((1, tm, w_), lambda b, i: (b, i, 0))
    consts = [gn, pg, pp]
    return pl.pallas_call(
        _combine_ple_kernel,
        grid=(B, S // tm),
        in_specs=[tok(D), tok(D), tok(D), tok(LANES), tok(PLE_DIM)] + [_const_spec(c.shape) for c in consts],
        out_specs=tok(D),
        out_shape=jax.ShapeDtypeStruct((B, S, D), F32),
        compiler_params=_params(("parallel", "parallel")),
        name="combine_ple",
    )(h, y0, y1, route, p, *consts)


def _block_diag_ones(n, blk):
    i = np.arange(n)
    return jnp.asarray((i[:, None] // blk == i[None, :] // blk).astype(np.float32), BF16)


def _prep_even(W, j, i):
    w = W['ab_w_in'][j]
    c = lambda a, b: w[:, a:b]
    k0, k1 = c(512, 576), c(576, 640)
    v0, v1 = c(640, 704), c(704, 768)
    w_in = jnp.concatenate([c(0, 512), k0, k0, k1, k1, v0, v0, v1, v1, c(768, 1280)], axis=1).astype(BF16)
    return dict(
        g=W['norm_mix'][i][None], w_in=w_in, e=_block_diag_ones(256, A_HEAD_DIM),
        qg=(jnp.tile(W['ab_q_norm'][j], A_HEADS) * (A_HEAD_DIM ** -0.5 * LOG2E))[None],
        kg=jnp.tile(W['ab_k_norm'][j], 2 * A_KV_HEADS)[None],
        sink=W['ab_sink'][j], pool_w=W['ab_pool_w'][j].astype(BF16), pool_scale=W['ab_pool_scale'][j][None],
        wo=W['ab_w_out'][j].astype(BF16), nf=W['norm_ffn'][i][None],
        wg=W['ffn_w_gate'][j].astype(BF16), wu=W['ffn_w_up'][j].astype(BF16), wd=W['ffn_w_down'][j].astype(BF16),
        gn=W['ple_gate_norm'][i][None], pg=W['ple_w_gate'][i].astype(BF16), pp=W['ple_w_proj'][i].astype(BF16))


def _prep_odd(W, j, i):
    pad_h = lambda a, n: jnp.pad(a, [(0, 0)] * (a.ndim - 1) + [(0, n - a.shape[-1])])
    w_in = pad_h(W['mla_w_in'][j], 512).astype(BF16)
    wuq = pad_h(W['mla_w_uq'][j].reshape(C_Q_LORA, C_HEADS, C_QK), LANES).reshape(C_Q_LORA, C_HEADS * LANES)
    gq = jnp.tile(pad_h(W['mla_q_norm'][j], LANES), C_HEADS) * (C_QK ** -0.5 * math.log2(math.e))
    ukv = W['mla_w_ukv'][j].reshape(C_KV_LORA, C_HEADS, C_NOPE + C_V)
    wuk = pad_h(ukv[..., :C_NOPE], LANES).reshape(C_KV_LORA, C_HEADS * LANES)
    wuv = ukv[..., C_NOPE:].reshape(C_KV_LORA, C_HEADS * C_V)
    kn = W['mla_k_norm'][j]
    gkn = jnp.tile(pad_h(kn[:C_NOPE], LANES), C_HEADS)[None]
    gkr = pad_h(kn[C_NOPE:], LANES)[None]
    half = C_ROPE // 2
    rot = np.zeros((LANES, LANES), np.float32)
    rot[np.arange(half) + half, np.arange(half)] = -1.0
    rot[np.arange(half), np.arange(half) + half] = 1.0
    place = np.zeros((LANES, C_HEADS * LANES), np.float32)
    for h in range(C_HEADS):
        place[np.arange(C_ROPE), h * LANES + C_NOPE + np.arange(C_ROPE)] = 1.0
    router = W['moe_router'][j]
    r_hi = router.astype(BF16)
    r_lo = (router - r_hi.astype(F32)).astype(BF16)
    r = pad_h(jnp.concatenate([r_hi, r_lo], axis=1), LANES)
    return dict(
        g=W['norm_mix'][i][None], w_in=w_in, qln=W['mla_q_lat_norm'][j][None], kln=W['mla_kv_lat_norm'][j][None],
        wuq=wuq.T.astype(BF16), gq=gq[:, None], wuk=wuk.astype(BF16), wuv=wuv.T.astype(BF16), gkn=gkn, gkr=gkr,
        rot=jnp.asarray(rot, BF16), place=jnp.asarray(place, BF16), e=_block_diag_ones(2 * LANES, LANES),
        wo=W['mla_w_out'][j].astype(BF16), nf=W['norm_ffn'][i][None], r=r,
        wg=W['moe_w_gate'][j].astype(BF16), wu=W['moe_w_up'][j].astype(BF16), wd=W['moe_w_down'][j].astype(BF16),
        gn=W['ple_gate_norm'][i][None], pg=W['ple_w_gate'][i].astype(BF16), pp=W['ple_w_proj'][i].astype(BF16))


def _rope_tables(S):
    half = C_ROPE // 2
    inv = ROPE_THETA ** (-jnp.arange(half, dtype=F32) / half)
    ang = jnp.arange(S).astype(F32)[:, None] * inv[None, :]
    cos, sin = jnp.cos(ang), jnp.sin(ang)
    wide = lambda t: jnp.pad(jnp.concatenate([t, t], axis=1), ((0, 0), (0, LANES - C_ROPE)))
    return cos.T, sin.T, wide(cos), wide(sin)


def _route_layout(ids, tm):
    n_slots = ids.size
    n_tiles = n_slots // tm + MOE_EXPERTS
    flat = ids.reshape(-1)
    order = jnp.argsort(flat, stable=True).astype(jnp.int32)
    rank = jnp.argsort(order).astype(jnp.int32)
    slot_end = jnp.searchsorted(flat[order], jnp.arange(MOE_EXPERTS, dtype=flat.dtype), side='right')
    slot_end = slot_end.astype(jnp.int32)
    counts = slot_end - jnp.concatenate([jnp.zeros((1,), jnp.int32), slot_end[:-1]])
    slot_start = slot_end - counts
    tiles = (counts + tm - 1) // tm
    tile_end = jnp.cumsum(tiles)
    tile_start = tile_end - tiles
    te = jnp.sum(jnp.arange(n_tiles)[:, None] >= tile_end[None, :], axis=1)
    tv = (te < MOE_EXPERTS).astype(jnp.int32)
    te = jnp.minimum(te, MOE_EXPERTS - 1).astype(jnp.int32)
    p = jnp.arange(n_tiles * tm, dtype=jnp.int32)
    e_p = jnp.repeat(te, tm)
    q = slot_start[e_p] + p - tile_start[e_p] * tm
    valid = (jnp.repeat(tv, tm) > 0) & (q < slot_end[e_p])
    src = jnp.where(valid, order[jnp.clip(q, 0, n_slots - 1)] // 2, 0)
    row_of_slot = tile_start[flat] * tm + rank - slot_start[flat]
    return src, te, tv, row_of_slot.reshape(ids.shape)


def _tiles(S):
    return dict(tm=min(512, S), tq_win=min(512, S), tq=min(ATT_TQ, S), tk=min(ATT_TK, S), te=512, nf=2)


def _trunk(x, p, even, odd):
    B, S, D = x.shape
    t = _tiles(S)
    q, k, v, u = _ab_in(x, even['g'], even['w_in'], even['e'], even['qg'], even['kg'], t['tm'])
    o = _win_pool(q, k, v, u, even['sink'], even['pool_w'], even['pool_scale'], t['tq_win'])
    h = _out_ffn_ple(x, o, p[0], even['wo'], even['nf'], even['wg'], even['wu'], even['wd'],
                     even['gn'], even['pg'], even['pp'], t['tm'])
    cosT, sinT, cos, sin = _rope_tables(S)
    qT, kk, vT = _mla_in(h, odd['g'], odd['w_in'], odd['qln'], odd['kln'], odd['wuq'], odd['gq'], cosT, sinT,
                         odd['wuk'], odd['wuv'], odd['gkn'], odd['gkr'], cos, sin, odd['rot'], odd['place'],
                         odd['e'], t['tm'])
    oT = _mla_attn(qT, kk, vT, t['tq'], t['tk'])
    h1, xn, route = _mla_out_router(h, oT.reshape(B, C_HEADS * C_V, S), odd['wo'], odd['nf'], odd['r'], t['tm'])
    ids = route[..., 0:2].astype(jnp.int32).reshape(B * S, 2)
    src, te, tv, rows = _route_layout(ids, t['te'])
    xs = jnp.take(xn.reshape(B * S, D), lax.optimization_barrier(src), axis=0)
    ys = _experts(xs, te, tv, odd['wg'], odd['wu'], odd['wd'], t['te'], t['nf'])
    y0 = jnp.take(ys, rows[:, 0], axis=0).reshape(B, S, D)
    y1 = jnp.take(ys, rows[:, 1], axis=0).reshape(B, S, D)
    return _combine_ple(h1, y0, y1, route, p[1], odd['gn'], odd['pg'], odd['pp'], t['tm'])


def kernel(x_prompt, x_sample, p_prompt, p_sample, norm_mix, norm_ffn, ab_w_in, ab_q_norm, ab_k_norm, ab_sink, ab_pool_w, ab_pool_scale, ab_w_out, ffn_w_gate, ffn_w_up, ffn_w_down, mla_w_in, mla_q_lat_norm, mla_kv_lat_norm, mla_w_uq, mla_w_ukv, mla_q_norm, mla_k_norm, mla_w_out, moe_router, moe_w_gate, moe_w_up, moe_w_down, ple_w_proj, ple_gate_norm, ple_w_gate):
    W = dict(norm_mix=norm_mix, norm_ffn=norm_ffn,
             ab_w_in=ab_w_in, ab_q_norm=ab_q_norm, ab_k_norm=ab_k_norm, ab_sink=ab_sink,
             ab_pool_w=ab_pool_w, ab_pool_scale=ab_pool_scale, ab_w_out=ab_w_out,
             ffn_w_gate=ffn_w_gate, ffn_w_up=ffn_w_up, ffn_w_down=ffn_w_down,
             mla_w_in=mla_w_in, mla_q_lat_norm=mla_q_lat_norm, mla_kv_lat_norm=mla_kv_lat_norm,
             mla_w_uq=mla_w_uq, mla_w_ukv=mla_w_ukv, mla_q_norm=mla_q_norm, mla_k_norm=mla_k_norm,
             mla_w_out=mla_w_out, moe_router=moe_router, moe_w_gate=moe_w_gate, moe_w_up=moe_w_up,
             moe_w_down=moe_w_down, ple_w_proj=ple_w_proj, ple_gate_norm=ple_gate_norm,
             ple_w_gate=ple_w_gate)
    even = _prep_even(W, 0, 0)
    odd = _prep_odd(W, 0, 1)
    return (_trunk(x_prompt, p_prompt, even, odd), _trunk(x_sample, p_sample, even, odd))
```
